```python
import math
import jax, jax.numpy as jnp
from jax import lax
import numpy as np

D_MODEL = 1024
BATCH = 4
SEQ = 8192
DEPTH = 1

CHUNK = 64
Q_BLOCK = 128
EPS = 1e-6
DA_HEADS = 8
DA_HEAD_DIM = D_MODEL // DA_HEADS // 2
DA_V_DIM = 2 * DA_HEAD_DIM
ROPE_THETA = 500000.0
ROPE_DIM = DA_HEAD_DIM // 4
SG_WIDTH = D_MODEL
SG_GROUPS = 8
SG_GROUP_DIM = SG_WIDTH // SG_GROUPS
SG_WINDOW = 128
MEM_LEN = 256
XA_HEADS = 4
XA_HEAD_DIM = D_MODEL // XA_HEADS
D_FF = 4 * D_MODEL
N_BRANCH = 2
Q_COLS = DA_HEADS * 2 * DA_HEAD_DIM
K_COLS = DA_HEADS * 2 * DA_HEAD_DIM
V_COLS = DA_HEADS * DA_V_DIM
SG_COLS = 2 * SG_WIDTH
GATE_COLS = N_BRANCH * D_MODEL
D_IN = Q_COLS + K_COLS + V_COLS + SG_COLS + GATE_COLS
SPLITS = (Q_COLS, Q_COLS + K_COLS, Q_COLS + K_COLS + V_COLS, Q_COLS + K_COLS + V_COLS + SG_COLS)

kernel_name = "hybrid_diffattn_gmlp_gated_block"


def rms_norm(x, g):
    xf = x.astype(jnp.float32)
    y = xf * lax.rsqrt(jnp.mean(xf * xf, axis=-1, keepdims=True) + EPS)
    return (y * g.astype(jnp.float32)).astype(x.dtype)


def layer_norm(x, g, b):
    xf = x.astype(jnp.float32)
    mu = jnp.mean(xf, axis=-1, keepdims=True)
    var = jnp.mean(jnp.square(xf - mu), axis=-1, keepdims=True)
    y = (xf - mu) * lax.rsqrt(var + 1e-5)
    return (y * g.astype(jnp.float32) + b.astype(jnp.float32)).astype(x.dtype)


def rope_tables(positions, dtype):
    idx = jnp.arange(0, ROPE_DIM, 2, dtype=jnp.float32)
    inv_freq = jnp.power(jnp.float32(ROPE_THETA), -idx / ROPE_DIM)
    ang = positions.astype(jnp.float32)[..., None] * inv_freq
    return (jnp.cos(ang)[:, :, None, None, :].astype(dtype),
            jnp.sin(ang)[:, :, None, None, :].astype(dtype))


def rope_partial(x, cos, sin):
    half = ROPE_DIM // 2
    x1 = x[..., :half]
    x2 = x[..., half:ROPE_DIM]
    rest = x[..., ROPE_DIM:]
    return jnp.concatenate([x1 * cos - x2 * sin, x2 * cos + x1 * sin, rest], axis=-1)


def diff_attention(q, k, v, lam):
    B, S = q.shape[0], q.shape[1]
    nb = S // Q_BLOCK
    scale = DA_HEAD_DIM ** -0.5
    k_chunk = jnp.arange(S) // CHUNK
    qb = q.reshape(B, nb, Q_BLOCK, DA_HEADS, 2, DA_HEAD_DIM).transpose(1, 0, 2, 3, 4, 5)

    def one_block(args):
        q_blk, blk = args
        q_chunk = (blk * Q_BLOCK + jnp.arange(Q_BLOCK)) // CHUNK
        mask = k_chunk[None, :] <= q_chunk[:, None]
        s = jnp.einsum('bqhmd,bkhmd->bhmqk', q_blk, k).astype(jnp.float32) * scale
        p = jax.nn.softmax(jnp.where(mask, s, -jnp.inf), axis=-1)
        a = p[:, :, 0] - lam * p[:, :, 1]
        return jnp.einsum('bhqk,bkhe->bqhe', a.astype(v.dtype), v)

    o = lax.map(one_block, (qb, jnp.arange(nb)))
    return o.transpose(1, 0, 2, 3, 4).reshape(B, S, DA_HEADS, DA_V_DIM)


def spatial_gate(z, ln_g, ln_b, w_s, b_s):
    B, S = z.shape[0], z.shape[1]
    u, v = jnp.split(z, 2, axis=-1)
    v = layer_norm(v, ln_g, ln_b)
    nw = S // SG_WINDOW
    v = v.reshape(B, nw, SG_WINDOW, SG_GROUPS, SG_GROUP_DIM)
    pos_chunk = jnp.arange(SG_WINDOW) // CHUNK
    mask = (pos_chunk[None, :] <= pos_chunk[:, None]).astype(w_s.dtype)
    v = jnp.einsum('gij,bwjgc->bwigc', w_s * mask[None], v) + b_s.T[None, None, :, :, None]
    return u * v.reshape(B, S, SG_WIDTH)


def cross_attention(h, m, w_q, w_kv, w_o):
    B, S = h.shape[0], h.shape[1]
    L = m.shape[1]
    q = (h @ w_q).reshape(B, S, XA_HEADS, XA_HEAD_DIM)
    k, v = jnp.split(m @ w_kv, 2, axis=-1)
    k = k.reshape(B, L, XA_HEADS, XA_HEAD_DIM)
    v = v.reshape(B, L, XA_HEADS, XA_HEAD_DIM)
    s = jnp.einsum('bqhd,bkhd->bhqk', q, k).astype(jnp.float32) * (XA_HEAD_DIM ** -0.5)
    p = jax.nn.softmax(s, axis=-1)
    o = jnp.einsum('bhqk,bkhd->bqhd', p.astype(v.dtype), v).reshape(B, S, D_MODEL)
    return o @ w_o


def setup_inputs(seed: int = 0) -> dict:
    key = jax.random.key(seed)
    ks = jax.random.split(key, 32)
    f32 = jnp.float32

    def nrm(k, shape, scale):
        return jax.random.normal(k, shape, f32) * scale

    def gain(k, shape):
        return 1.0 + 0.02 * jax.random.normal(k, shape, f32)

    x = jax.random.normal(ks[0], (BATCH, SEQ, D_MODEL), f32)
    mem = jax.random.normal(ks[1], (BATCH, MEM_LEN, D_MODEL), f32)
    start = jax.random.randint(ks[2], (BATCH, 1), 0, 64, dtype=jnp.int32) * CHUNK
    positions = (start + jnp.arange(SEQ, dtype=jnp.int32)[None, :]).astype(jnp.int32)
    L = DEPTH
    return {
        "x": x,
        "mem": mem,
        "positions": positions,
        "g_mix": gain(ks[3], (L, D_MODEL)),
        "w_in": nrm(ks[4], (L, D_MODEL, D_IN), D_MODEL ** -0.5),
        "lam_q1": nrm(ks[5], (L, DA_HEAD_DIM), 0.1),
        "lam_k1": nrm(ks[6], (L, DA_HEAD_DIM), 0.1),
        "lam_q2": nrm(ks[7], (L, DA_HEAD_DIM), 0.1),
        "lam_k2": nrm(ks[8], (L, DA_HEAD_DIM), 0.1),
        "g_subln": gain(ks[9], (L, DA_V_DIM)),
        "sg_ln_g": gain(ks[10], (L, SG_WIDTH)),
        "sg_ln_b": nrm(ks[11], (L, SG_WIDTH), 0.02),
        "sg_w": nrm(ks[12], (L, SG_GROUPS, SG_WINDOW, SG_WINDOW), SG_WINDOW ** -0.5),
        "sg_b": 1.0 + nrm(ks[13], (L, SG_GROUPS, SG_WINDOW), 0.01),
        "w_branch_attn": nrm(ks[14], (L, V_COLS, D_MODEL), V_COLS ** -0.5),
        "w_branch_sg": nrm(ks[15], (L, SG_WIDTH, D_MODEL), SG_WIDTH ** -0.5),
        "w_out": nrm(ks[16], (L, D_MODEL, D_MODEL), D_MODEL ** -0.5),
        "g_xa": gain(ks[17], (L, D_MODEL)),
        "g_mem": gain(ks[18], (L, D_MODEL)),
        "w_xq": nrm(ks[19], (L, D_MODEL, D_MODEL), D_MODEL ** -0.5),
        "w_xkv": nrm(ks[20], (L, D_MODEL, 2 * D_MODEL), D_MODEL ** -0.5),
        "w_xo": nrm(ks[21], (L, D_MODEL, D_MODEL), D_MODEL ** -0.5),
        "g_ffn": gain(ks[22], (L, D_MODEL)),
        "w_ff1": nrm(ks[23], (L, D_MODEL, D_FF), D_MODEL ** -0.5),
        "w_ff2": nrm(ks[24], (L, D_FF, D_MODEL), D_FF ** -0.5),
        "g_final": gain(ks[25], (D_MODEL,)),
    }


def reference(x, mem, positions, g_mix, w_in, lam_q1, lam_k1, lam_q2, lam_k2, g_subln,
              sg_ln_g, sg_ln_b, sg_w, sg_b, w_branch_attn, w_branch_sg, w_out,
              g_xa, g_mem, w_xq, w_xkv, w_xo, g_ffn, w_ff1, w_ff2, g_final):
    B, S = x.shape[0], x.shape[1]
    cos, sin = rope_tables(positions, x.dtype)
    h = x
    for l in range(DEPTH):
        n = rms_norm(h, g_mix[l])
        proj = n @ w_in[l]
        q, k, v, z, g = jnp.split(proj, SPLITS, axis=-1)
        q = rope_partial(q.reshape(B, S, DA_HEADS, 2, DA_HEAD_DIM), cos, sin)
        k = rope_partial(k.reshape(B, S, DA_HEADS, 2, DA_HEAD_DIM), cos, sin)
        v = v.reshape(B, S, DA_HEADS, DA_V_DIM)
        lam_init = 0.8 - 0.6 * math.exp(-0.3 * l)
        lam = (jnp.exp(jnp.sum(lam_q1[l].astype(jnp.float32) * lam_k1[l].astype(jnp.float32)))
               - jnp.exp(jnp.sum(lam_q2[l].astype(jnp.float32) * lam_k2[l].astype(jnp.float32)))
               + lam_init)
        a = diff_attention(q, k, v, lam)
        a = (rms_norm(a, g_subln[l]) * (1.0 - lam_init)).reshape(B, S, V_COLS)
        sgo = spatial_gate(jax.nn.gelu(z), sg_ln_g[l], sg_ln_b[l], sg_w[l], sg_b[l])
        g_a, g_s = jnp.split(jax.nn.sigmoid(g), N_BRANCH, axis=-1)
        merged = g_a * (a @ w_branch_attn[l]) + g_s * (sgo @ w_branch_sg[l])
        h = h + merged @ w_out[l]
        h = h + cross_attention(rms_norm(h, g_xa[l]), rms_norm(mem, g_mem[l]),
                                w_xq[l], w_xkv[l], w_xo[l])
        f = rms_norm(h, g_ffn[l]) @ w_ff1[l]
        h = h + jnp.square(jax.nn.relu(f)) @ w_ff2[l]
    return rms_norm(h, g_final)
```

```python
import functools
import math

import jax
import jax.numpy as jnp
from jax import lax
from jax.experimental import pallas as pl
from jax.experimental.pallas import tpu as pltpu

F32 = jnp.float32
BF16 = jnp.bfloat16

D_MODEL = 1024
CHUNK = 64
EPS = 1e-6
LN_EPS = 1e-5
DA_HEADS = 8
DA_HEAD_DIM = 64
DA_V_DIM = 128
ROPE_THETA = 500000.0
ROPE_DIM = 16
ROPE_HALF = ROPE_DIM // 2
SG_GROUPS = 8
SG_GROUP_DIM = 128
SG_WINDOW = 128
XA_HEADS = 4
XA_HEAD_DIM = 256
D_FF = 4096
D_IN = 7 * D_MODEL
SEG_Q, SEG_K, SEG_V, SEG_U, SEG_SV, SEG_GA, SEG_GS = range(7)

LANES = 128
VMEM_LIMIT = 56 * 1024 * 1024


def _rms(x, g):
    return x * lax.rsqrt(jnp.mean(x * x, axis=-1, keepdims=True) + EPS) * g


def _params(sem, vmem=VMEM_LIMIT):
    return pltpu.CompilerParams(dimension_semantics=sem, vmem_limit_bytes=vmem)


def _in_proj_kernel(x_ref, pos_ref, invf_ref, g_ref, w_ref, o_ref, n_scr, c_scr, s1_scr, s2_scr):
    j = pl.program_id(1)

    @pl.when(j == 0)
    def _():
        n_scr[...] = _rms(x_ref[...], g_ref[...]).astype(BF16)
        ang = pos_ref[...] * invf_ref[...]
        d = lax.broadcasted_iota(jnp.int32, (1, LANES), 1) % DA_HEAD_DIM
        lo = d < ROPE_HALF
        hi = (d >= ROPE_HALF) & (d < ROPE_DIM)
        cos = jnp.cos(ang)
        sin = jnp.sin(ang)
        c_scr[...] = jnp.where(lo | hi, cos, 1.0)
        s1_scr[...] = jnp.where(lo, -sin, 0.0)
        s2_scr[...] = jnp.where(hi, sin, 0.0)

    acc = jnp.dot(n_scr[...], w_ref[...], preferred_element_type=F32)

    def rope(scale):
        c, s1, s2 = c_scr[...], s1_scr[...], s2_scr[...]
        for h in range(DA_HEADS):
            sl = acc[:, h * LANES:(h + 1) * LANES]
            r = sl * c + pltpu.roll(sl, LANES - ROPE_HALF, 1) * s1 + pltpu.roll(sl, ROPE_HALF, 1) * s2
            o_ref[:, h * LANES:(h + 1) * LANES] = (r * scale).astype(o_ref.dtype)

    @pl.when(j == SEG_Q)
    def _():
        rope(DA_HEAD_DIM ** -0.5)

    @pl.when(j == SEG_K)
    def _():
        rope(1.0)

    @pl.when(j == SEG_V)
    def _():
        o_ref[...] = acc.astype(o_ref.dtype)

    @pl.when((j == SEG_U) | (j == SEG_SV))
    def _():
        o_ref[...] = jax.nn.gelu(acc).astype(o_ref.dtype)

    @pl.when(j >= SEG_GA)
    def _():
        o_ref[...] = jax.nn.sigmoid(acc).astype(o_ref.dtype)


def _in_proj(x2, pos, invf, g, w, tm):
    T = x2.shape[0]
    tn = D_MODEL
    return pl.pallas_call(
        _in_proj_kernel,
        grid=(T // tm, D_IN // tn),
        in_specs=[
            pl.BlockSpec((tm, D_MODEL), lambda i, j: (i, 0)),
            pl.BlockSpec((tm, 1), lambda i, j: (i, 0)),
            pl.BlockSpec((1, LANES), lambda i, j: (0, 0)),
            pl.BlockSpec((1, D_MODEL), lambda i, j: (0, 0)),
            pl.BlockSpec((D_MODEL, tn), lambda i, j: (0, j)),
        ],
        out_specs=pl.BlockSpec((tm, tn), lambda i, j: (i, j)),
        out_shape=jax.ShapeDtypeStruct((T, D_IN), BF16),
        scratch_shapes=[
            pltpu.VMEM((tm, D_MODEL), BF16),
            pltpu.VMEM((tm, LANES), F32),
            pltpu.VMEM((tm, LANES), F32),
            pltpu.VMEM((tm, LANES), F32),
        ],
        compiler_params=_params(("arbitrary", "arbitrary")),
        name="in_proj",
    )(x2, pos, invf, g, w)


def _attn_kernel(lq1_ref, lk1_ref, lq2_ref, lk2_ref, gsub_ref, q_ref, k_ref, v_ref, o_ref,
                 m_scr, l_scr, acc_scr, *, bq, lam_init):
    qi = pl.program_id(2)
    q = q_ref[...]
    lane = lax.broadcasted_iota(jnp.int32, (1, LANES), 1)
    zero = jnp.zeros_like(q)
    qs = jnp.concatenate([jnp.where(lane < DA_HEAD_DIM, q, zero),
                          jnp.where(lane >= DA_HEAD_DIM, q, zero)], axis=0)

    m_scr[...] = jnp.full(m_scr.shape, -jnp.inf, F32)
    l_scr[...] = jnp.zeros(l_scr.shape, F32)
    acc_scr[...] = jnp.zeros(acc_scr.shape, F32)

    def step(kb, masked):
        start = pl.multiple_of(kb * bq, bq)
        k = k_ref[pl.ds(start, bq), :]
        v = v_ref[pl.ds(start, bq), :]
        s = lax.dot_general(qs, k, (((1,), (1,)), ((), ())), preferred_element_type=F32)
        if masked:
            r = lax.broadcasted_iota(jnp.int32, (2 * bq, bq), 0) % bq
            c = lax.broadcasted_iota(jnp.int32, (2 * bq, bq), 1)
            s = jnp.where(c // CHUNK <= r // CHUNK, s, -jnp.inf)
        m_prev = m_scr[...]
        m_new = jnp.maximum(m_prev, jnp.max(s, axis=-1, keepdims=True))
        alpha = jnp.exp(m_prev - m_new)
        p = jnp.exp(s - m_new)
        l_scr[...] = alpha * l_scr[...] + jnp.sum(p, axis=-1, keepdims=True)
        acc_scr[...] = alpha * acc_scr[...] + jnp.dot(p.astype(BF16), v, preferred_element_type=F32)
        m_scr[...] = m_new

    def body(kb, carry):
        step(kb, False)
        return carry

    lax.fori_loop(0, qi, body, 0)
    step(qi, True)

    lam = (jnp.exp(jnp.sum(lq1_ref[...] * lk1_ref[...], keepdims=True))
           - jnp.exp(jnp.sum(lq2_ref[...] * lk2_ref[...], keepdims=True)) + lam_init)
    o = acc_scr[...] / l_scr[...]
    a = o[:bq] - lam * o[bq:]
    o_ref[...] = (_rms(a, gsub_ref[...]) * (1.0 - lam_init)).astype(o_ref.dtype)


def _attn(proj, lq1, lk1, lq2, lk2, gsub, B, S, bq, lam_init):
    T = B * S
    nq = S // bq
    small = lambda shape: pl.BlockSpec(shape, lambda b, h, i: (0, 0))
    return pl.pallas_call(
        functools.partial(_attn_kernel, bq=bq, lam_init=lam_init),
        grid=(B, DA_HEADS, nq),
        in_specs=[
            small((1, DA_HEAD_DIM)), small((1, DA_HEAD_DIM)), small((1, DA_HEAD_DIM)), small((1, DA_HEAD_DIM)),
            small((1, DA_V_DIM)),
            pl.BlockSpec((bq, LANES), lambda b, h, i: (b * nq + i, SEG_Q * DA_HEADS + h)),
            pl.BlockSpec((S, LANES), lambda b, h, i: (b, SEG_K * DA_HEADS + h)),
            pl.BlockSpec((S, LANES), lambda b, h, i: (b, SEG_V * DA_HEADS + h)),
        ],
        out_specs=pl.BlockSpec((bq, LANES), lambda b, h, i: (b * nq + i, h)),
        out_shape=jax.ShapeDtypeStruct((T, D_MODEL), BF16),
        scratch_shapes=[
            pltpu.VMEM((2 * bq, 1), F32),
            pltpu.VMEM((2 * bq, 1), F32),
            pltpu.VMEM((2 * bq, DA_V_DIM), F32),
        ],
        compiler_params=_params(("arbitrary", "arbitrary", "arbitrary")),
        name="attn",
    )(lq1, lk1, lq2, lk2, gsub, proj, proj, proj)


def _sgate_kernel(u_ref, v_ref, g_ref, b_ref, w_ref, bias_ref, o_ref, *, tm):
    v = v_ref[...].astype(F32)
    mu = jnp.mean(v, axis=-1, keepdims=True)
    vc = v - mu
    var = jnp.mean(vc * vc, axis=-1, keepdims=True)
    vn = (vc * lax.rsqrt(var + LN_EPS) * g_ref[...] + b_ref[...]).astype(BF16)
    i = lax.broadcasted_iota(jnp.int32, (SG_WINDOW, SG_WINDOW), 0) // CHUNK
    j = lax.broadcasted_iota(jnp.int32, (SG_WINDOW, SG_WINDOW), 1) // CHUNK
    keep = j <= i
    for g in range(SG_GROUPS):
        wm = jnp.where(keep, w_ref[g], 0.0).astype(BF16)
        cols = slice(g * SG_GROUP_DIM, (g + 1) * SG_GROUP_DIM)
        for w in range(tm // SG_WINDOW):
            rows = slice(w * SG_WINDOW, (w + 1) * SG_WINDOW)
            mixed = jnp.dot(wm, vn[rows, cols], preferred_element_type=F32) + bias_ref[:, cols]
            o_ref[rows, cols] = (u_ref[rows, cols].astype(F32) * mixed).astype(o_ref.dtype)


def _sgate(proj, ln_g, ln_b, w_s, bias, tm):
    T = proj.shape[0]
    return pl.pallas_call(
        functools.partial(_sgate_kernel, tm=tm),
        grid=(T // tm,),
        in_specs=[
            pl.BlockSpec((tm, D_MODEL), lambda i: (i, SEG_U)),
            pl.BlockSpec((tm, D_MODEL), lambda i: (i, SEG_SV)),
            pl.BlockSpec((1, D_MODEL), lambda i: (0, 0)),
            pl.BlockSpec((1, D_MODEL), lambda i: (0, 0)),
            pl.BlockSpec((SG_GROUPS, SG_WINDOW, SG_WINDOW), lambda i: (0, 0, 0)),
            pl.BlockSpec((SG_WINDOW, D_MODEL), lambda i: (0, 0)),
        ],
        out_specs=pl.BlockSpec((tm, D_MODEL), lambda i: (i, 0)),
        out_shape=jax.ShapeDtypeStruct((T, D_MODEL), BF16),
        compiler_params=_params(("arbitrary",)),
        name="sgate",
    )(proj, proj, ln_g, ln_b, w_s, bias)


def _merge_kernel(x_ref, a_ref, s_ref, ga_ref, gs_ref, wa_ref, ws_ref, wo_ref, o_ref):
    ba = jnp.dot(a_ref[...], wa_ref[...], preferred_element_type=F32)
    bs = jnp.dot(s_ref[...], ws_ref[...], preferred_element_type=F32)
    merged = ga_ref[...].astype(F32) * ba + gs_ref[...].astype(F32) * bs
    o_ref[...] = x_ref[...] + jnp.dot(merged.astype(BF16), wo_ref[...], preferred_element_type=F32)


def _merge(x2, a, sgo, proj, wa, ws, wo, tm):
    T = x2.shape[0]
    row = lambda c: pl.BlockSpec((tm, D_MODEL), lambda i: (i, c))
    wspec = pl.BlockSpec((D_MODEL, D_MODEL), lambda i: (0, 0))
    return pl.pallas_call(
        _merge_kernel,
        grid=(T // tm,),
        in_specs=[row(0), row(0), row(0), row(SEG_GA), row(SEG_GS), wspec, wspec, wspec],
        out_specs=row(0),
        out_shape=jax.ShapeDtypeStruct((T, D_MODEL), F32),
        compiler_params=_params(("arbitrary",)),
        name="merge",
    )(x2, a, sgo, proj, proj, wa, ws, wo)


def _memkv_kernel(m_ref, g_ref, w_ref, o_ref):
    mn = _rms(m_ref[...], g_ref[...]).astype(BF16)
    o_ref[...] = jnp.dot(mn, w_ref[...], preferred_element_type=F32).astype(o_ref.dtype)


def _memkv(mem2, g, w, L):
    R = mem2.shape[0]
    return pl.pallas_call(
        _memkv_kernel,
        grid=(R // L,),
        in_specs=[
            pl.BlockSpec((L, D_MODEL), lambda i: (i, 0)),
            pl.BlockSpec((1, D_MODEL), lambda i: (0, 0)),
            pl.BlockSpec((D_MODEL, 2 * D_MODEL), lambda i: (0, 0)),
        ],
        out_specs=pl.BlockSpec((L, 2 * D_MODEL), lambda i: (i, 0)),
        out_shape=jax.ShapeDtypeStruct((R, 2 * D_MODEL), BF16),
        compiler_params=_params(("arbitrary",)),
        name="memkv",
    )(mem2, g, w)


def _xattn_kernel(h_ref, g_ref, wq_ref, k_ref, v_ref, wo_ref, o_ref):
    h = h_ref[...]
    hn = _rms(h, g_ref[...]).astype(BF16)
    q = (jnp.dot(hn, wq_ref[...], preferred_element_type=F32) * XA_HEAD_DIM ** -0.5).astype(BF16)
    outs = []
    for hd in range(XA_HEADS):
        cols = slice(hd * XA_HEAD_DIM, (hd + 1) * XA_HEAD_DIM)
        s = lax.dot_general(q[:, cols], k_ref[:, cols], (((1,), (1,)), ((), ())), preferred_element_type=F32)
        p = jnp.exp(s - jnp.max(s, axis=-1, keepdims=True))
        l = jnp.sum(p, axis=-1, keepdims=True)
        pv = jnp.dot(p.astype(BF16), v_ref[:, cols], preferred_element_type=F32)
        outs.append((pv / l).astype(BF16))
    o = jnp.concatenate(outs, axis=-1)
    o_ref[...] = h + jnp.dot(o, wo_ref[...], preferred_element_type=F32)


def _xattn(h1, g, wq, kv, wo, S, L, tm):
    T = h1.shape[0]
    per_b = S // tm
    wspec = pl.BlockSpec((D_MODEL, D_MODEL), lambda i: (0, 0))
    return pl.pallas_call(
        _xattn_kernel,
        grid=(T // tm,),
        in_specs=[
            pl.BlockSpec((tm, D_MODEL), lambda i: (i, 0)),
            pl.BlockSpec((1, D_MODEL), lambda i: (0, 0)),
            wspec,
            pl.BlockSpec((L, D_MODEL), lambda i: (i // per_b, 0)),
            pl.BlockSpec((L, D_MODEL), lambda i: (i // per_b, 1)),
            wspec,
        ],
        out_specs=pl.BlockSpec((tm, D_MODEL), lambda i: (i, 0)),
        out_shape=jax.ShapeDtypeStruct((T, D_MODEL), F32),
        compiler_params=_params(("arbitrary",)),
        name="xattn",
    )(h1, g, wq, kv, kv, wo)


def _ffn_kernel(h_ref, g_ref, w1_ref, w2_ref, gf_ref, o_ref, n_scr, acc_scr, *, final):
    j = pl.program_id(1)

    @pl.when(j == 0)
    def _():
        n_scr[...] = _rms(h_ref[...], g_ref[...]).astype(BF16)
        acc_scr[...] = h_ref[...]

    f = jnp.maximum(jnp.dot(n_scr[...], w1_ref[...], preferred_element_type=F32), 0.0)
    acc_scr[...] += jnp.dot((f * f).astype(BF16), w2_ref[...], preferred_element_type=F32)

    @pl.when(j == pl.num_programs(1) - 1)
    def _():
        o_ref[...] = _rms(acc_scr[...], gf_ref[...]) if final else acc_scr[...]


def _ffn(h2, g, w1, w2, gf, tm, tf, final):
    T = h2.shape[0]
    return pl.pallas_call(
        functools.partial(_ffn_kernel, final=final),
        grid=(T // tm, D_FF // tf),
        in_specs=[
            pl.BlockSpec((tm, D_MODEL), lambda i, j: (i, 0)),
            pl.BlockSpec((1, D_MODEL), lambda i, j: (0, 0)),
            pl.BlockSpec((D_MODEL, tf), lambda i, j: (0, j)),
            pl.BlockSpec((tf, D_MODEL), lambda i, j: (j, 0)),
            pl.BlockSpec((1, D_MODEL), lambda i, j: (0, 0)),
        ],
        out_specs=pl.BlockSpec((tm, D_MODEL), lambda i, j: (i, 0)),
        out_shape=jax.ShapeDtypeStruct((T, D_MODEL), F32),
        scratch_shapes=[pltpu.VMEM((tm, D_MODEL), BF16), pltpu.VMEM((tm, D_MODEL), F32)],
        compiler_params=_params(("arbitrary", "arbitrary")),
        name="ffn",
    )(h2, g, w1, w2, gf)


def _tile(n, pref):
    t = min(n, pref)
    assert n % t == 0, (n, t)
    return t


def kernel(x, mem, positions, g_mix, w_in, lam_q1, lam_k1, lam_q2, lam_k2, g_subln, sg_ln_g, sg_ln_b, sg_w, sg_b, w_branch_attn, w_branch_sg, w_out, g_xa, g_mem, w_xq, w_xkv, w_xo, g_ffn, w_ff1, w_ff2, g_final):
    B, S, _ = x.shape
    L = mem.shape[1]
    depth = w_in.shape[0]
    T = B * S
    tm = _tile(S, 512)
    bq = _tile(S, 256)

    idx = jnp.arange(0, ROPE_DIM, 2, dtype=F32)
    inv_freq = jnp.power(jnp.float32(ROPE_THETA), -idx / ROPE_DIM)
    d = jnp.arange(LANES) % DA_HEAD_DIM
    invf = jnp.where(d < ROPE_DIM, inv_freq[d % ROPE_HALF], 0.0).reshape(1, LANES)
    pos = positions.reshape(T, 1).astype(F32)

    row = lambda p: p.reshape(1, -1).astype(F32)
    h = x.reshape(T, D_MODEL)
    mem2 = mem.reshape(B * L, D_MODEL)
    for l in range(depth):
        lam_init = 0.8 - 0.6 * math.exp(-0.3 * l)
        proj = _in_proj(h, pos, invf, row(g_mix[l]), w_in[l].astype(BF16), tm)
        a = _attn(proj, row(lam_q1[l]), row(lam_k1[l]), row(lam_q2[l]), row(lam_k2[l]), row(g_subln[l]),
                  B, S, bq, lam_init)
        bias = jnp.repeat(sg_b[l].T.astype(F32), SG_GROUP_DIM, axis=1)
        sgo = _sgate(proj, row(sg_ln_g[l]), row(sg_ln_b[l]), sg_w[l].astype(F32), bias, tm)
        h1 = _merge(h, a, sgo, proj, w_branch_attn[l].astype(BF16), w_branch_sg[l].astype(BF16),
                    w_out[l].astype(BF16), tm)
        kv = _memkv(mem2, row(g_mem[l]), w_xkv[l].astype(BF16), L)
        h2 = _xattn(h1, row(g_xa[l]), w_xq[l].astype(BF16), kv, w_xo[l].astype(BF16), S, L, tm)
        h = _ffn(h2, row(g_ffn[l]), w_ff1[l].astype(BF16), w_ff2[l].astype(BF16), row(g_final),
                 tm, _tile(D_FF, 1024), final=(l == depth - 1))
    return h.reshape(B, S, D_MODEL)
```

```python
import functools
import math

import jax
import jax.numpy as jnp
from jax import lax
from jax.experimental import pallas as pl
from jax.experimental.pallas import tpu as pltpu

F32 = jnp.float32
BF16 = jnp.bfloat16

D_MODEL = 1024
CHUNK = 64
EPS = 1e-6
LN_EPS = 1e-5
DA_HEADS = 8
DA_HEAD_DIM = 64
DA_V_DIM = 128
ROPE_THETA = 500000.0
ROPE_DIM = 16
ROPE_HALF = ROPE_DIM // 2
SG_GROUPS = 8
SG_GROUP_DIM = 128
SG_WINDOW = 128
XA_HEADS = 4
XA_HEAD_DIM = 256
D_FF = 4096
N_SEG = 6
SEG_Q, SEG_K, SEG_U, SEG_SV, SEG_GA, SEG_GS = range(N_SEG)

LANES = 128
VMEM_LIMIT = 56 * 1024 * 1024


def _rms(x, g):
    return x * lax.rsqrt(jnp.mean(x * x, axis=-1, keepdims=True) + EPS) * g


def _params(sem, vmem=VMEM_LIMIT):
    return pltpu.CompilerParams(dimension_semantics=sem, vmem_limit_bytes=vmem)


def _in_proj_kernel(x_ref, pos_ref, invf_ref, g_ref, w_ref, o_ref, n_scr, c_scr, s1_scr, s2_scr):
    j = pl.program_id(1)

    @pl.when(j == 0)
    def _():
        n_scr[...] = _rms(x_ref[...], g_ref[...]).astype(BF16)
        ang = pos_ref[...] * invf_ref[...]
        d = lax.broadcasted_iota(jnp.int32, (1, LANES), 1) % DA_HEAD_DIM
        lo = d < ROPE_HALF
        hi = (d >= ROPE_HALF) & (d < ROPE_DIM)
        cos = jnp.cos(ang)
        sin = jnp.sin(ang)
        c_scr[...] = jnp.where(lo | hi, cos, 1.0)
        s1_scr[...] = jnp.where(lo, -sin, 0.0)
        s2_scr[...] = jnp.where(hi, sin, 0.0)

    acc = jnp.dot(n_scr[...], w_ref[...], preferred_element_type=F32)

    def rope(scale):
        c, s1, s2 = c_scr[...], s1_scr[...], s2_scr[...]
        for h in range(DA_HEADS):
            sl = acc[:, h * LANES:(h + 1) * LANES]
            r = sl * c + pltpu.roll(sl, LANES - ROPE_HALF, 1) * s1 + pltpu.roll(sl, ROPE_HALF, 1) * s2
            o_ref[:, h * LANES:(h + 1) * LANES] = (r * scale).astype(o_ref.dtype)

    @pl.when(j == SEG_Q)
    def _():
        rope(DA_HEAD_DIM ** -0.5)

    @pl.when(j == SEG_K)
    def _():
        rope(1.0)

    @pl.when((j == SEG_U) | (j == SEG_SV))
    def _():
        o_ref[...] = jax.nn.gelu(acc).astype(o_ref.dtype)

    @pl.when(j >= SEG_GA)
    def _():
        o_ref[...] = jax.nn.sigmoid(acc).astype(o_ref.dtype)


def _in_proj(x2, pos, invf, g, w, tm):
    T = x2.shape[0]
    tn = D_MODEL
    return pl.pallas_call(
        _in_proj_kernel,
        grid=(T // tm, N_SEG),
        in_specs=[
            pl.BlockSpec((tm, D_MODEL), lambda i, j: (i, 0)),
            pl.BlockSpec((tm, 1), lambda i, j: (i, 0)),
            pl.BlockSpec((1, LANES), lambda i, j: (0, 0)),
            pl.BlockSpec((1, D_MODEL), lambda i, j: (0, 0)),
            pl.BlockSpec((D_MODEL, tn), lambda i, j: (0, j)),
        ],
        out_specs=pl.BlockSpec((tm, tn), lambda i, j: (i, j)),
        out_shape=jax.ShapeDtypeStruct((T, N_SEG * D_MODEL), BF16),
        scratch_shapes=[
            pltpu.VMEM((tm, D_MODEL), BF16),
            pltpu.VMEM((tm, LANES), F32),
            pltpu.VMEM((tm, LANES), F32),
            pltpu.VMEM((tm, LANES), F32),
        ],
        compiler_params=_params(("arbitrary", "arbitrary")),
        name="in_proj",
    )(x2, pos, invf, g, w)


def _v_proj_kernel(x_ref, g_ref, wt_ref, o_ref):
    n = _rms(x_ref[...], g_ref[...]).astype(BF16)
    vt = lax.dot_general(wt_ref[...], n, (((1,), (1,)), ((), ())), preferred_element_type=F32)
    for h in range(DA_HEADS):
        o_ref[h, 0] = vt[h * DA_V_DIM:(h + 1) * DA_V_DIM, :].astype(o_ref.dtype)


def _v_proj(x2, g, wt, B, S, bk):
    nk = S // bk
    return pl.pallas_call(
        _v_proj_kernel,
        grid=(B * nk,),
        in_specs=[
            pl.BlockSpec((bk, D_MODEL), lambda i: (i, 0)),
            pl.BlockSpec((1, D_MODEL), lambda i: (0, 0)),
            pl.BlockSpec((D_MODEL, D_MODEL), lambda i: (0, 0)),
        ],
        out_specs=pl.BlockSpec((DA_HEADS, 1, DA_V_DIM, bk), lambda i: (i // nk, i % nk, 0, 0)),
        out_shape=jax.ShapeDtypeStruct((B * DA_HEADS, nk, DA_V_DIM, bk), BF16),
        compiler_params=_params(("arbitrary",)),
        name="v_proj",
    )(x2, g, wt)


def _attn_kernel(lq1_ref, lk1_ref, lq2_ref, lk2_ref, gsub_ref, q_ref, k_ref, vt_ref, o_ref,
                 m_scr, l_scr, acc_scr, *, bq, lam_init):
    qi = pl.program_id(2)
    q = q_ref[...]
    lane = lax.broadcasted_iota(jnp.int32, (1, LANES), 1)
    zero = jnp.zeros_like(q)
    qs = jnp.concatenate([jnp.where(lane < DA_HEAD_DIM, q, zero),
                          jnp.where(lane >= DA_HEAD_DIM, q, zero)], axis=0)

    m_scr[...] = jnp.full(m_scr.shape, -jnp.inf, F32)
    l_scr[...] = jnp.zeros(l_scr.shape, F32)
    acc_scr[...] = jnp.zeros(acc_scr.shape, F32)

    def step(kb, masked):
        start = pl.multiple_of(kb * bq, bq)
        k = k_ref[pl.ds(start, bq), :]
        st = lax.dot_general(k, qs, (((1,), (1,)), ((), ())), preferred_element_type=F32)
        if masked:
            key = lax.broadcasted_iota(jnp.int32, (bq, 2 * bq), 0)
            qry = lax.broadcasted_iota(jnp.int32, (bq, 2 * bq), 1) % bq
            st = jnp.where(key // CHUNK <= qry // CHUNK, st, -jnp.inf)
        m_prev = m_scr[...]
        m_new = jnp.maximum(m_prev, jnp.max(st, axis=0, keepdims=True))
        alpha = jnp.exp(m_prev - m_new)
        p = jnp.exp(st - m_new)
        l_scr[...] = alpha * l_scr[...] + jnp.sum(p, axis=0, keepdims=True)
        acc_scr[...] = alpha * acc_scr[...] + jnp.dot(vt_ref[0, kb], p.astype(BF16),
                                                      preferred_element_type=F32)
        m_scr[...] = m_new

    def body(kb, carry):
        step(kb, False)
        return carry

    lax.fori_loop(0, qi, body, 0)
    step(qi, True)

    lam = (jnp.exp(jnp.sum(lq1_ref[...] * lk1_ref[...], keepdims=True))
           - jnp.exp(jnp.sum(lq2_ref[...] * lk2_ref[...], keepdims=True)) + lam_init)
    ot = acc_scr[...] / l_scr[...]
    at = ot[:, :bq] - lam * ot[:, bq:]
    ms = jnp.mean(at * at, axis=0, keepdims=True)
    at = at * lax.rsqrt(ms + EPS) * gsub_ref[...] * (1.0 - lam_init)
    o_ref[...] = at.T.astype(o_ref.dtype)


def _attn(proj, vt, lq1, lk1, lq2, lk2, gsub, B, S, bq, lam_init):
    T = B * S
    nq = S // bq
    small = lambda shape: pl.BlockSpec(shape, lambda b, h, i: (0, 0))
    return pl.pallas_call(
        functools.partial(_attn_kernel, bq=bq, lam_init=lam_init),
        grid=(B, DA_HEADS, nq),
        in_specs=[
            small((1, DA_HEAD_DIM)), small((1, DA_HEAD_DIM)), small((1, DA_HEAD_DIM)), small((1, DA_HEAD_DIM)),
            small((DA_V_DIM, 1)),
            pl.BlockSpec((bq, LANES), lambda b, h, i: (b * nq + i, SEG_Q * DA_HEADS + h)),
            pl.BlockSpec((S, LANES), lambda b, h, i: (b, SEG_K * DA_HEADS + h)),
            pl.BlockSpec((1, nq, DA_V_DIM, bq), lambda b, h, i: (b * DA_HEADS + h, 0, 0, 0)),
        ],
        out_specs=pl.BlockSpec((bq, LANES), lambda b, h, i: (b * nq + i, h)),
        out_shape=jax.ShapeDtypeStruct((T, D_MODEL), BF16),
        scratch_shapes=[
            pltpu.VMEM((1, 2 * bq), F32),
            pltpu.VMEM((1, 2 * bq), F32),
            pltpu.VMEM((DA_V_DIM, 2 * bq), F32),
        ],
        compiler_params=_params(("arbitrary", "arbitrary", "arbitrary")),
        name="attn",
    )(lq1, lk1, lq2, lk2, gsub, proj, proj, vt)


def _sgate_kernel(u_ref, v_ref, g_ref, b_ref, w_ref, bias_ref, o_ref, *, tm):
    v = v_ref[...].astype(F32)
    mu = jnp.mean(v, axis=-1, keepdims=True)
    vc = v - mu
    var = jnp.mean(vc * vc, axis=-1, keepdims=True)
    vn = (vc * lax.rsqrt(var + LN_EPS) * g_ref[...] + b_ref[...]).astype(BF16)
    i = lax.broadcasted_iota(jnp.int32, (SG_WINDOW, SG_WINDOW), 0) // CHUNK
    j = lax.broadcasted_iota(jnp.int32, (SG_WINDOW, SG_WINDOW), 1) // CHUNK
    keep = j <= i
    for g in range(SG_GROUPS):
        wm = jnp.where(keep, w_ref[g], 0.0).astype(BF16)
        cols = slice(g * SG_GROUP_DIM, (g + 1) * SG_GROUP_DIM)
        for w in range(tm // SG_WINDOW):
            rows = slice(w * SG_WINDOW, (w + 1) * SG_WINDOW)
            mixed = jnp.dot(wm, vn[rows, cols], preferred_element_type=F32) + bias_ref[:, cols]
            o_ref[rows, cols] = (u_ref[rows, cols].astype(F32) * mixed).astype(o_ref.dtype)


def _sgate(proj, ln_g, ln_b, w_s, bias, tm):
    T = proj.shape[0]
    return pl.pallas_call(
        functools.partial(_sgate_kernel, tm=tm),
        grid=(T // tm,),
        in_specs=[
            pl.BlockSpec((tm, D_MODEL), lambda i: (i, SEG_U)),
            pl.BlockSpec((tm, D_MODEL), lambda i: (i, SEG_SV)),
            pl.BlockSpec((1, D_MODEL), lambda i: (0, 0)),
            pl.BlockSpec((1, D_MODEL), lambda i: (0, 0)),
            pl.BlockSpec((SG_GROUPS, SG_WINDOW, SG_WINDOW), lambda i: (0, 0, 0)),
            pl.BlockSpec((SG_WINDOW, D_MODEL), lambda i: (0, 0)),
        ],
        out_specs=pl.BlockSpec((tm, D_MODEL), lambda i: (i, 0)),
        out_shape=jax.ShapeDtypeStruct((T, D_MODEL), BF16),
        compiler_params=_params(("arbitrary",)),
        name="sgate",
    )(proj, proj, ln_g, ln_b, w_s, bias)


def _merge_kernel(x_ref, a_ref, s_ref, ga_ref, gs_ref, wa_ref, ws_ref, wo_ref, o_ref):
    ba = jnp.dot(a_ref[...], wa_ref[...], preferred_element_type=F32)
    bs = jnp.dot(s_ref[...], ws_ref[...], preferred_element_type=F32)
    merged = ga_ref[...].astype(F32) * ba + gs_ref[...].astype(F32) * bs
    o_ref[...] = x_ref[...] + jnp.dot(merged.astype(BF16), wo_ref[...], preferred_element_type=F32)


def _merge(x2, a, sgo, proj, wa, ws, wo, tm):
    T = x2.shape[0]
    row = lambda c: pl.BlockSpec((tm, D_MODEL), lambda i: (i, c))
    wspec = pl.BlockSpec((D_MODEL, D_MODEL), lambda i: (0, 0))
    return pl.pallas_call(
        _merge_kernel,
        grid=(T // tm,),
        in_specs=[row(0), row(0), row(0), row(SEG_GA), row(SEG_GS), wspec, wspec, wspec],
        out_specs=row(0),
        out_shape=jax.ShapeDtypeStruct((T, D_MODEL), F32),
        compiler_params=_params(("arbitrary",)),
        name="merge",
    )(x2, a, sgo, proj, proj, wa, ws, wo)


def _memkv_kernel(m_ref, g_ref, w_ref, o_ref):
    mn = _rms(m_ref[...], g_ref[...]).astype(BF16)
    o_ref[...] = jnp.dot(mn, w_ref[...], preferred_element_type=F32).astype(o_ref.dtype)


def _memkv(mem2, g, w, L):
    R = mem2.shape[0]
    return pl.pallas_call(
        _memkv_kernel,
        grid=(R // L,),
        in_specs=[
            pl.BlockSpec((L, D_MODEL), lambda i: (i, 0)),
            pl.BlockSpec((1, D_MODEL), lambda i: (0, 0)),
            pl.BlockSpec((D_MODEL, 2 * D_MODEL), lambda i: (0, 0)),
        ],
        out_specs=pl.BlockSpec((L, 2 * D_MODEL), lambda i: (i, 0)),
        out_shape=jax.ShapeDtypeStruct((R, 2 * D_MODEL), BF16),
        compiler_params=_params(("arbitrary",)),
        name="memkv",
    )(mem2, g, w)


def _xattn_kernel(h_ref, g_ref, wq_ref, k_ref, v_ref, wo_ref, o_ref):
    h = h_ref[...]
    hn = _rms(h, g_ref[...]).astype(BF16)
    q = (jnp.dot(hn, wq_ref[...], preferred_element_type=F32) * XA_HEAD_DIM ** -0.5).astype(BF16)
    outs = []
    for hd in range(XA_HEADS):
        cols = slice(hd * XA_HEAD_DIM, (hd + 1) * XA_HEAD_DIM)
        s = lax.dot_general(q[:, cols], k_ref[:, cols], (((1,), (1,)), ((), ())), preferred_element_type=F32)
        p = jnp.exp(s - jnp.max(s, axis=-1, keepdims=True))
        l = jnp.sum(p, axis=-1, keepdims=True)
        pv = jnp.dot(p.astype(BF16), v_ref[:, cols], preferred_element_type=F32)
        outs.append((pv / l).astype(BF16))
    o = jnp.concatenate(outs, axis=-1)
    o_ref[...] = h + jnp.dot(o, wo_ref[...], preferred_element_type=F32)


def _xattn(h1, g, wq, kv, wo, S, L, tm):
    T = h1.shape[0]
    per_b = S // tm
    wspec = pl.BlockSpec((D_MODEL, D_MODEL), lambda i: (0, 0))
    return pl.pallas_call(
        _xattn_kernel,
        grid=(T // tm,),
        in_specs=[
            pl.BlockSpec((tm, D_MODEL), lambda i: (i, 0)),
            pl.BlockSpec((1, D_MODEL), lambda i: (0, 0)),
            wspec,
            pl.BlockSpec((L, D_MODEL), lambda i: (i // per_b, 0)),
            pl.BlockSpec((L, D_MODEL), lambda i: (i // per_b, 1)),
            wspec,
        ],
        out_specs=pl.BlockSpec((tm, D_MODEL), lambda i: (i, 0)),
        out_shape=jax.ShapeDtypeStruct((T, D_MODEL), F32),
        compiler_params=_params(("arbitrary",)),
        name="xattn",
    )(h1, g, wq, kv, kv, wo)


def _ffn_kernel(h_ref, g_ref, w1_ref, w2_ref, gf_ref, o_ref, n_scr, acc_scr, *, final):
    j = pl.program_id(1)

    @pl.when(j == 0)
    def _():
        n_scr[...] = _rms(h_ref[...], g_ref[...]).astype(BF16)
        acc_scr[...] = h_ref[...]

    f = jnp.maximum(jnp.dot(n_scr[...], w1_ref[...], preferred_element_type=F32), 0.0)
    acc_scr[...] += jnp.dot((f * f).astype(BF16), w2_ref[...], preferred_element_type=F32)

    @pl.when(j == pl.num_programs(1) - 1)
    def _():
        o_ref[...] = _rms(acc_scr[...], gf_ref[...]) if final else acc_scr[...]


def _ffn(h2, g, w1, w2, gf, tm, tf, final):
    T = h2.shape[0]
    return pl.pallas_call(
        functools.partial(_ffn_kernel, final=final),
        grid=(T // tm, D_FF // tf),
        in_specs=[
            pl.BlockSpec((tm, D_MODEL), lambda i, j: (i, 0)),
            pl.BlockSpec((1, D_MODEL), lambda i, j: (0, 0)),
            pl.BlockSpec((D_MODEL, tf), lambda i, j: (0, j)),
            pl.BlockSpec((tf, D_MODEL), lambda i, j: (j, 0)),
            pl.BlockSpec((1, D_MODEL), lambda i, j: (0, 0)),
        ],
        out_specs=pl.BlockSpec((tm, D_MODEL), lambda i, j: (i, 0)),
        out_shape=jax.ShapeDtypeStruct((T, D_MODEL), F32),
        scratch_shapes=[pltpu.VMEM((tm, D_MODEL), BF16), pltpu.VMEM((tm, D_MODEL), F32)],
        compiler_params=_params(("arbitrary", "arbitrary")),
        name="ffn",
    )(h2, g, w1, w2, gf)


def _tile(n, pref):
    t = min(n, pref)
    assert n % t == 0, (n, t)
    return t


def kernel(x, mem, positions, g_mix, w_in, lam_q1, lam_k1, lam_q2, lam_k2, g_subln, sg_ln_g, sg_ln_b, sg_w, sg_b, w_branch_attn, w_branch_sg, w_out, g_xa, g_mem, w_xq, w_xkv, w_xo, g_ffn, w_ff1, w_ff2, g_final):
    B, S, _ = x.shape
    L = mem.shape[1]
    depth = w_in.shape[0]
    T = B * S
    tm = _tile(S, 512)
    bq = _tile(S, 512)

    idx = jnp.arange(0, ROPE_DIM, 2, dtype=F32)
    inv_freq = jnp.power(jnp.float32(ROPE_THETA), -idx / ROPE_DIM)
    d = jnp.arange(LANES) % DA_HEAD_DIM
    invf = jnp.where(d < ROPE_DIM, inv_freq[d % ROPE_HALF], 0.0).reshape(1, LANES)
    pos = positions.reshape(T, 1).astype(F32)

    row = lambda p: p.reshape(1, -1).astype(F32)
    h = x.reshape(T, D_MODEL)
    mem2 = mem.reshape(B * L, D_MODEL)
    for l in range(depth):
        lam_init = 0.8 - 0.6 * math.exp(-0.3 * l)
        w = w_in[l].astype(BF16)
        w_rows = jnp.concatenate([w[:, :2 * D_MODEL], w[:, 3 * D_MODEL:]], axis=1)
        w_vt = w[:, 2 * D_MODEL:3 * D_MODEL].T
        proj = _in_proj(h, pos, invf, row(g_mix[l]), w_rows, tm)
        vt = _v_proj(h, row(g_mix[l]), w_vt, B, S, bq)
        a = _attn(proj, vt, row(lam_q1[l]), row(lam_k1[l]), row(lam_q2[l]), row(lam_k2[l]),
                  g_subln[l].reshape(DA_V_DIM, 1).astype(F32), B, S, bq, lam_init)
        bias = jnp.repeat(sg_b[l].T.astype(F32), SG_GROUP_DIM, axis=1)
        sgo = _sgate(proj, row(sg_ln_g[l]), row(sg_ln_b[l]), sg_w[l].astype(F32), bias, tm)
        h1 = _merge(h, a, sgo, proj, w_branch_attn[l].astype(BF16), w_branch_sg[l].astype(BF16),
                    w_out[l].astype(BF16), tm)
        kv = _memkv(mem2, row(g_mem[l]), w_xkv[l].astype(BF16), L)
        h2 = _xattn(h1, row(g_xa[l]), w_xq[l].astype(BF16), kv, w_xo[l].astype(BF16), S, L, tm)
        h = _ffn(h2, row(g_ffn[l]), w_ff1[l].astype(BF16), w_ff2[l].astype(BF16), row(g_final),
                 tm, _tile(D_FF, 1024), final=(l == depth - 1))
    return h.reshape(B, S, D_MODEL)
```

```python
import functools
import math

import jax
import jax.numpy as jnp
from jax import lax
from jax.experimental import pallas as pl
from jax.experimental.pallas import tpu as pltpu

F32 = jnp.float32
BF16 = jnp.bfloat16

D_MODEL = 1024
CHUNK = 64
EPS = 1e-6
LN_EPS = 1e-5
DA_HEADS = 8
DA_HEAD_DIM = 64
DA_V_DIM = 128
ROPE_THETA = 500000.0
ROPE_DIM = 16
ROPE_HALF = ROPE_DIM // 2
SG_GROUPS = 8
SG_GROUP_DIM = 128
SG_WINDOW = 128
XA_HEADS = 4
XA_HEAD_DIM = 256
D_FF = 4096
N_SEG = 6
SEG_Q, SEG_K, SEG_U, SEG_SV, SEG_GA, SEG_GS = range(N_SEG)

LANES = 128
VMEM_LIMIT = 56 * 1024 * 1024


def _rms(x, g):
    return x * lax.rsqrt(jnp.mean(x * x, axis=-1, keepdims=True) + EPS) * g


def _params(sem, vmem=VMEM_LIMIT):
    return pltpu.CompilerParams(dimension_semantics=sem, vmem_limit_bytes=vmem)


def _in_proj_kernel(x_ref, pos_ref, invf_ref, g_ref, w_ref, o_ref, n_scr, c_scr, s1_scr, s2_scr):
    j = pl.program_id(1)

    @pl.when(j == 0)
    def _():
        n_scr[...] = _rms(x_ref[...], g_ref[...]).astype(BF16)
        ang = pos_ref[...] * invf_ref[...]
        d = lax.broadcasted_iota(jnp.int32, (1, LANES), 1) % DA_HEAD_DIM
        lo = d < ROPE_HALF
        hi = (d >= ROPE_HALF) & (d < ROPE_DIM)
        cos = jnp.cos(ang)
        sin = jnp.sin(ang)
        c_scr[...] = jnp.where(lo | hi, cos, 1.0)
        s1_scr[...] = jnp.where(lo, -sin, 0.0)
        s2_scr[...] = jnp.where(hi, sin, 0.0)

    acc = jnp.dot(n_scr[...], w_ref[...], preferred_element_type=F32)

    def rope(scale):
        c, s1, s2 = c_scr[...], s1_scr[...], s2_scr[...]
        for h in range(DA_HEADS):
            sl = acc[:, h * LANES:(h + 1) * LANES]
            r = sl * c + pltpu.roll(sl, LANES - ROPE_HALF, 1) * s1 + pltpu.roll(sl, ROPE_HALF, 1) * s2
            o_ref[:, h * LANES:(h + 1) * LANES] = (r * scale).astype(o_ref.dtype)

    @pl.when(j == SEG_Q)
    def _():
        rope(DA_HEAD_DIM ** -0.5 * math.log2(math.e))

    @pl.when(j == SEG_K)
    def _():
        rope(1.0)

    @pl.when((j == SEG_U) | (j == SEG_SV))
    def _():
        o_ref[...] = jax.nn.gelu(acc).astype(o_ref.dtype)

    @pl.when(j >= SEG_GA)
    def _():
        o_ref[...] = jax.nn.sigmoid(acc).astype(o_ref.dtype)


def _in_proj(x2, pos, invf, g, w, tm):
    T = x2.shape[0]
    tn = D_MODEL
    return pl.pallas_call(
        _in_proj_kernel,
        grid=(T // tm, N_SEG),
        in_specs=[
            pl.BlockSpec((tm, D_MODEL), lambda i, j: (i, 0)),
            pl.BlockSpec((tm, 1), lambda i, j: (i, 0)),
            pl.BlockSpec((1, LANES), lambda i, j: (0, 0)),
            pl.BlockSpec((1, D_MODEL), lambda i, j: (0, 0)),
            pl.BlockSpec((D_MODEL, tn), lambda i, j: (0, j)),
        ],
        out_specs=pl.BlockSpec((tm, tn), lambda i, j: (i, j)),
        out_shape=jax.ShapeDtypeStruct((T, N_SEG * D_MODEL), BF16),
        scratch_shapes=[
            pltpu.VMEM((tm, D_MODEL), BF16),
            pltpu.VMEM((tm, LANES), F32),
            pltpu.VMEM((tm, LANES), F32),
            pltpu.VMEM((tm, LANES), F32),
        ],
        compiler_params=_params(("arbitrary", "arbitrary")),
        name="in_proj",
    )(x2, pos, invf, g, w)


def _v_proj_kernel(x_ref, g_ref, wt_ref, o_ref):
    n = _rms(x_ref[...], g_ref[...]).astype(BF16)
    vt = lax.dot_general(wt_ref[...], n, (((1,), (1,)), ((), ())), preferred_element_type=F32)
    for h in range(DA_HEADS):
        o_ref[h, 0] = vt[h * DA_V_DIM:(h + 1) * DA_V_DIM, :].astype(o_ref.dtype)


def _v_proj(x2, g, wt, B, S, bk):
    nk = S // bk
    return pl.pallas_call(
        _v_proj_kernel,
        grid=(B * nk,),
        in_specs=[
            pl.BlockSpec((bk, D_MODEL), lambda i: (i, 0)),
            pl.BlockSpec((1, D_MODEL), lambda i: (0, 0)),
            pl.BlockSpec((D_MODEL, D_MODEL), lambda i: (0, 0)),
        ],
        out_specs=pl.BlockSpec((DA_HEADS, 1, DA_V_DIM, bk), lambda i: (i // nk, i % nk, 0, 0)),
        out_shape=jax.ShapeDtypeStruct((B * DA_HEADS, nk, DA_V_DIM, bk), BF16),
        compiler_params=_params(("arbitrary",)),
        name="v_proj",
    )(x2, g, wt)


def _attn_kernel(lq1_ref, lk1_ref, lq2_ref, lk2_ref, gsub_ref, q_ref, k_ref, vt_ref, o_ref,
                 sa_scr, sb_scr, mca_scr, mcb_scr, m_scr, l_scr, acc_scr, *, bq, lam_init):
    qi = pl.program_id(2)
    q = q_ref[...]
    lane = lax.broadcasted_iota(jnp.int32, (1, LANES), 1)
    zero = jnp.zeros_like(q)
    qs = jnp.concatenate([jnp.where(lane < DA_HEAD_DIM, q, zero),
                          jnp.where(lane >= DA_HEAD_DIM, q, zero)], axis=0)

    m_scr[...] = jnp.full(m_scr.shape, -jnp.inf, F32)
    l_scr[...] = jnp.zeros(l_scr.shape, F32)
    acc_scr[...] = jnp.zeros(acc_scr.shape, F32)

    def scores(kb, buf, masked):
        s_ref, mc_ref = buf
        start = pl.multiple_of(kb * bq, bq)
        k = k_ref[pl.ds(start, bq), :]
        st = lax.dot_general(k, qs, (((1,), (1,)), ((), ())), preferred_element_type=F32)
        if masked:
            key = lax.broadcasted_iota(jnp.int32, (bq, 2 * bq), 0)
            qry = lax.broadcasted_iota(jnp.int32, (bq, 2 * bq), 1) % bq
            st = jnp.where(key // CHUNK <= qry // CHUNK, st, -jnp.inf)
        s_ref[...] = st
        mc_ref[...] = jnp.max(st, axis=0, keepdims=True)

    def accumulate(kb, buf):
        s_ref, mc_ref = buf
        m_prev = m_scr[...]
        m_new = jnp.maximum(m_prev, mc_ref[...])
        alpha = jnp.exp2(m_prev - m_new)
        p = jnp.exp2(s_ref[...] - m_new)
        l_scr[...] = alpha * l_scr[...] + jnp.sum(p, axis=0, keepdims=True)
        acc_scr[...] = alpha * acc_scr[...] + jnp.dot(vt_ref[0, kb], p.astype(BF16),
                                                      preferred_element_type=F32)
        m_scr[...] = m_new

    buf_a, buf_b = (sa_scr, mca_scr), (sb_scr, mcb_scr)
    scores(qi, buf_a, True)

    def pair(u, carry):
        t = 2 * u
        scores(t, buf_b, False)
        accumulate(jnp.where(t == 0, qi, t - 1), buf_a)
        scores(t + 1, buf_a, False)
        accumulate(t, buf_b)
        return carry

    lax.fori_loop(0, qi // 2, pair, 0)

    @pl.when(qi % 2 == 1)
    def _():
        t = qi - 1
        scores(t, buf_b, False)
        accumulate(jnp.where(t == 0, qi, t - 1), buf_a)
        accumulate(t, buf_b)

    @pl.when(qi % 2 == 0)
    def _():
        accumulate(jnp.where(qi == 0, qi, qi - 1), buf_a)

    lam = (jnp.exp(jnp.sum(lq1_ref[...] * lk1_ref[...], keepdims=True))
           - jnp.exp(jnp.sum(lq2_ref[...] * lk2_ref[...], keepdims=True)) + lam_init)
    ot = acc_scr[...] / l_scr[...]
    at = ot[:, :bq] - lam * ot[:, bq:]
    ms = jnp.mean(at * at, axis=0, keepdims=True)
    at = at * lax.rsqrt(ms + EPS) * gsub_ref[...] * (1.0 - lam_init)
    o_ref[...] = at.T.astype(o_ref.dtype)


def _attn(proj, vt, lq1, lk1, lq2, lk2, gsub, B, S, bq, lam_init):
    T = B * S
    nq = S // bq
    small = lambda shape: pl.BlockSpec(shape, lambda b, h, i: (0, 0))
    return pl.pallas_call(
        functools.partial(_attn_kernel, bq=bq, lam_init=lam_init),
        grid=(B, DA_HEADS, nq),
        in_specs=[
            small((1, DA_HEAD_DIM)), small((1, DA_HEAD_DIM)), small((1, DA_HEAD_DIM)), small((1, DA_HEAD_DIM)),
            small((DA_V_DIM, 1)),
            pl.BlockSpec((bq, LANES), lambda b, h, i: (b * nq + i, SEG_Q * DA_HEADS + h)),
            pl.BlockSpec((S, LANES), lambda b, h, i: (b, SEG_K * DA_HEADS + h)),
            pl.BlockSpec((1, nq, DA_V_DIM, bq), lambda b, h, i: (b * DA_HEADS + h, 0, 0, 0)),
        ],
        out_specs=pl.BlockSpec((bq, LANES), lambda b, h, i: (b * nq + i, h)),
        out_shape=jax.ShapeDtypeStruct((T, D_MODEL), BF16),
        scratch_shapes=[
            pltpu.VMEM((bq, 2 * bq), F32),
            pltpu.VMEM((bq, 2 * bq), F32),
            pltpu.VMEM((1, 2 * bq), F32),
            pltpu.VMEM((1, 2 * bq), F32),
            pltpu.VMEM((1, 2 * bq), F32),
            pltpu.VMEM((1, 2 * bq), F32),
            pltpu.VMEM((DA_V_DIM, 2 * bq), F32),
        ],
        compiler_params=_params(("arbitrary", "arbitrary", "arbitrary")),
        name="attn",
    )(lq1, lk1, lq2, lk2, gsub, proj, proj, vt)


def _sgate_kernel(u_ref, v_ref, g_ref, b_ref, w_ref, bias_ref, o_ref, *, tm):
    v = v_ref[...].astype(F32)
    mu = jnp.mean(v, axis=-1, keepdims=True)
    vc = v - mu
    var = jnp.mean(vc * vc, axis=-1, keepdims=True)
    vn = (vc * lax.rsqrt(var + LN_EPS) * g_ref[...] + b_ref[...]).astype(BF16)
    i = lax.broadcasted_iota(jnp.int32, (SG_WINDOW, SG_WINDOW), 0) // CHUNK
    j = lax.broadcasted_iota(jnp.int32, (SG_WINDOW, SG_WINDOW), 1) // CHUNK
    keep = j <= i
    for g in range(SG_GROUPS):
        wm = jnp.where(keep, w_ref[g], 0.0).astype(BF16)
        cols = slice(g * SG_GROUP_DIM, (g + 1) * SG_GROUP_DIM)
        for w in range(tm // SG_WINDOW):
            rows = slice(w * SG_WINDOW, (w + 1) * SG_WINDOW)
            mixed = jnp.dot(wm, vn[rows, cols], preferred_element_type=F32) + bias_ref[:, cols]
            o_ref[rows, cols] = (u_ref[rows, cols].astype(F32) * mixed).astype(o_ref.dtype)


def _sgate(proj, ln_g, ln_b, w_s, bias, tm):
    T = proj.shape[0]
    return pl.pallas_call(
        functools.partial(_sgate_kernel, tm=tm),
        grid=(T // tm,),
        in_specs=[
            pl.BlockSpec((tm, D_MODEL), lambda i: (i, SEG_U)),
            pl.BlockSpec((tm, D_MODEL), lambda i: (i, SEG_SV)),
            pl.BlockSpec((1, D_MODEL), lambda i: (0, 0)),
            pl.BlockSpec((1, D_MODEL), lambda i: (0, 0)),
            pl.BlockSpec((SG_GROUPS, SG_WINDOW, SG_WINDOW), lambda i: (0, 0, 0)),
            pl.BlockSpec((SG_WINDOW, D_MODEL), lambda i: (0, 0)),
        ],
        out_specs=pl.BlockSpec((tm, D_MODEL), lambda i: (i, 0)),
        out_shape=jax.ShapeDtypeStruct((T, D_MODEL), BF16),
        compiler_params=_params(("arbitrary",)),
        name="sgate",
    )(proj, proj, ln_g, ln_b, w_s, bias)


def _merge_kernel(x_ref, a_ref, s_ref, ga_ref, gs_ref, wa_ref, ws_ref, wo_ref, o_ref):
    ba = jnp.dot(a_ref[...], wa_ref[...], preferred_element_type=F32)
    bs = jnp.dot(s_ref[...], ws_ref[...], preferred_element_type=F32)
    merged = ga_ref[...].astype(F32) * ba + gs_ref[...].astype(F32) * bs
    o_ref[...] = x_ref[...] + jnp.dot(merged.astype(BF16), wo_ref[...], preferred_element_type=F32)


def _merge(x2, a, sgo, proj, wa, ws, wo, tm):
    T = x2.shape[0]
    row = lambda c: pl.BlockSpec((tm, D_MODEL), lambda i: (i, c))
    wspec = pl.BlockSpec((D_MODEL, D_MODEL), lambda i: (0, 0))
    return pl.pallas_call(
        _merge_kernel,
        grid=(T // tm,),
        in_specs=[row(0), row(0), row(0), row(SEG_GA), row(SEG_GS), wspec, wspec, wspec],
        out_specs=row(0),
        out_shape=jax.ShapeDtypeStruct((T, D_MODEL), F32),
        compiler_params=_params(("arbitrary",)),
        name="merge",
    )(x2, a, sgo, proj, proj, wa, ws, wo)


def _memkv_kernel(m_ref, g_ref, w_ref, o_ref):
    mn = _rms(m_ref[...], g_ref[...]).astype(BF16)
    o_ref[...] = jnp.dot(mn, w_ref[...], preferred_element_type=F32).astype(o_ref.dtype)


def _memkv(mem2, g, w, L):
    R = mem2.shape[0]
    return pl.pallas_call(
        _memkv_kernel,
        grid=(R // L,),
        in_specs=[
            pl.BlockSpec((L, D_MODEL), lambda i: (i, 0)),
            pl.BlockSpec((1, D_MODEL), lambda i: (0, 0)),
            pl.BlockSpec((D_MODEL, 2 * D_MODEL), lambda i: (0, 0)),
        ],
        out_specs=pl.BlockSpec((L, 2 * D_MODEL), lambda i: (i, 0)),
        out_shape=jax.ShapeDtypeStruct((R, 2 * D_MODEL), BF16),
        compiler_params=_params(("arbitrary",)),
        name="memkv",
    )(mem2, g, w)


def _xattn_kernel(h_ref, g_ref, wq_ref, k_ref, v_ref, wo_ref, o_ref):
    h = h_ref[...]
    hn = _rms(h, g_ref[...]).astype(BF16)
    q = (jnp.dot(hn, wq_ref[...], preferred_element_type=F32) * XA_HEAD_DIM ** -0.5).astype(BF16)
    outs = []
    for hd in range(XA_HEADS):
        cols = slice(hd * XA_HEAD_DIM, (hd + 1) * XA_HEAD_DIM)
        s = lax.dot_general(q[:, cols], k_ref[:, cols], (((1,), (1,)), ((), ())), preferred_element_type=F32)
        p = jnp.exp(s - jnp.max(s, axis=-1, keepdims=True))
        l = jnp.sum(p, axis=-1, keepdims=True)
        pv = jnp.dot(p.astype(BF16), v_ref[:, cols], preferred_element_type=F32)
        outs.append((pv / l).astype(BF16))
    o = jnp.concatenate(outs, axis=-1)
    o_ref[...] = h + jnp.dot(o, wo_ref[...], preferred_element_type=F32)


def _xattn(h1, g, wq, kv, wo, S, L, tm):
    T = h1.shape[0]
    per_b = S // tm
    wspec = pl.BlockSpec((D_MODEL, D_MODEL), lambda i: (0, 0))
    return pl.pallas_call(
        _xattn_kernel,
        grid=(T // tm,),
        in_specs=[
            pl.BlockSpec((tm, D_MODEL), lambda i: (i, 0)),
            pl.BlockSpec((1, D_MODEL), lambda i: (0, 0)),
            wspec,
            pl.BlockSpec((L, D_MODEL), lambda i: (i // per_b, 0)),
            pl.BlockSpec((L, D_MODEL), lambda i: (i // per_b, 1)),
            wspec,
        ],
        out_specs=pl.BlockSpec((tm, D_MODEL), lambda i: (i, 0)),
        out_shape=jax.ShapeDtypeStruct((T, D_MODEL), F32),
        compiler_params=_params(("arbitrary",)),
        name="xattn",
    )(h1, g, wq, kv, kv, wo)


def _ffn_kernel(h_ref, g_ref, w1_ref, w2_ref, gf_ref, o_ref, n_scr, acc_scr, *, final):
    j = pl.program_id(1)

    @pl.when(j == 0)
    def _():
        n_scr[...] = _rms(h_ref[...], g_ref[...]).astype(BF16)
        acc_scr[...] = h_ref[...]

    f = jnp.maximum(jnp.dot(n_scr[...], w1_ref[...], preferred_element_type=F32), 0.0)
    acc_scr[...] += jnp.dot((f * f).astype(BF16), w2_ref[...], preferred_element_type=F32)

    @pl.when(j == pl.num_programs(1) - 1)
    def _():
        o_ref[...] = _rms(acc_scr[...], gf_ref[...]) if final else acc_scr[...]


def _ffn(h2, g, w1, w2, gf, tm, tf, final):
    T = h2.shape[0]
    return pl.pallas_call(
        functools.partial(_ffn_kernel, final=final),
        grid=(T // tm, D_FF // tf),
        in_specs=[
            pl.BlockSpec((tm, D_MODEL), lambda i, j: (i, 0)),
            pl.BlockSpec((1, D_MODEL), lambda i, j: (0, 0)),
            pl.BlockSpec((D_MODEL, tf), lambda i, j: (0, j)),
            pl.BlockSpec((tf, D_MODEL), lambda i, j: (j, 0)),
            pl.BlockSpec((1, D_MODEL), lambda i, j: (0, 0)),
        ],
        out_specs=pl.BlockSpec((tm, D_MODEL), lambda i, j: (i, 0)),
        out_shape=jax.ShapeDtypeStruct((T, D_MODEL), F32),
        scratch_shapes=[pltpu.VMEM((tm, D_MODEL), BF16), pltpu.VMEM((tm, D_MODEL), F32)],
        compiler_params=_params(("arbitrary", "arbitrary")),
        name="ffn",
    )(h2, g, w1, w2, gf)


def _tile(n, pref):
    t = min(n, pref)
    assert n % t == 0, (n, t)
    return t


def kernel(x, mem, positions, g_mix, w_in, lam_q1, lam_k1, lam_q2, lam_k2, g_subln, sg_ln_g, sg_ln_b, sg_w, sg_b, w_branch_attn, w_branch_sg, w_out, g_xa, g_mem, w_xq, w_xkv, w_xo, g_ffn, w_ff1, w_ff2, g_final):
    B, S, _ = x.shape
    L = mem.shape[1]
    depth = w_in.shape[0]
    T = B * S
    tm = _tile(S, 512)
    bq = _tile(S, 512)

    idx = jnp.arange(0, ROPE_DIM, 2, dtype=F32)
    inv_freq = jnp.power(jnp.float32(ROPE_THETA), -idx / ROPE_DIM)
    d = jnp.arange(LANES) % DA_HEAD_DIM
    invf = jnp.where(d < ROPE_DIM, inv_freq[d % ROPE_HALF], 0.0).reshape(1, LANES)
    pos = positions.reshape(T, 1).astype(F32)

    row = lambda p: p.reshape(1, -1).astype(F32)
    h = x.reshape(T, D_MODEL)
    mem2 = mem.reshape(B * L, D_MODEL)
    for l in range(depth):
        lam_init = 0.8 - 0.6 * math.exp(-0.3 * l)
        w = w_in[l].astype(BF16)
        w_rows = jnp.concatenate([w[:, :2 * D_MODEL], w[:, 3 * D_MODEL:]], axis=1)
        w_vt = w[:, 2 * D_MODEL:3 * D_MODEL].T
        proj = _in_proj(h, pos, invf, row(g_mix[l]), w_rows, tm)
        vt = _v_proj(h, row(g_mix[l]), w_vt, B, S, bq)
        a = _attn(proj, vt, row(lam_q1[l]), row(lam_k1[l]), row(lam_q2[l]), row(lam_k2[l]),
                  g_subln[l].reshape(DA_V_DIM, 1).astype(F32), B, S, bq, lam_init)
        bias = jnp.repeat(sg_b[l].T.astype(F32), SG_GROUP_DIM, axis=1)
        sgo = _sgate(proj, row(sg_ln_g[l]), row(sg_ln_b[l]), sg_w[l].astype(F32), bias, tm)
        h1 = _merge(h, a, sgo, proj, w_branch_attn[l].astype(BF16), w_branch_sg[l].astype(BF16),
                    w_out[l].astype(BF16), tm)
        kv = _memkv(mem2, row(g_mem[l]), w_xkv[l].astype(BF16), L)
        h2 = _xattn(h1, row(g_xa[l]), w_xq[l].astype(BF16), kv, w_xo[l].astype(BF16), S, L, tm)
        h = _ffn(h2, row(g_ffn[l]), w_ff1[l].astype(BF16), w_ff2[l].astype(BF16), row(g_final),
                 tm, _tile(D_FF, 1024), final=(l == depth - 1))
    return h.reshape(B, S, D_MODEL)
```

```python
import functools
import math

import jax
import jax.numpy as jnp
from jax import lax
from jax.experimental import pallas as pl
from jax.experimental.pallas import tpu as pltpu

F32 = jnp.float32
BF16 = jnp.bfloat16

D_MODEL = 1024
CHUNK = 64
EPS = 1e-6
LN_EPS = 1e-5
DA_HEADS = 8
DA_HEAD_DIM = 64
DA_V_DIM = 128
ROPE_THETA = 500000.0
ROPE_DIM = 16
ROPE_HALF = ROPE_DIM // 2
SG_GROUPS = 8
SG_GROUP_DIM = 128
SG_WINDOW = 128
XA_HEADS = 4
XA_HEAD_DIM = 256
D_FF = 4096
N_SEG = 6
SEG_Q, SEG_K, SEG_U, SEG_SV, SEG_GA, SEG_GS = range(N_SEG)

LANES = 128
VMEM_LIMIT = 56 * 1024 * 1024


def _rms(x, g):
    return x * lax.rsqrt(jnp.mean(x * x, axis=-1, keepdims=True) + EPS) * g


def _params(sem, vmem=VMEM_LIMIT):
    return pltpu.CompilerParams(dimension_semantics=sem, vmem_limit_bytes=vmem)


def _in_proj_kernel(x_ref, pos_ref, invf_ref, g_ref, w_ref, wvt_ref, o_ref, vt_ref):
    n = _rms(x_ref[...], g_ref[...]).astype(BF16)
    ang = pos_ref[...] * invf_ref[...]
    d = lax.broadcasted_iota(jnp.int32, (1, LANES), 1) % DA_HEAD_DIM
    lo = d < ROPE_HALF
    hi = (d >= ROPE_HALF) & (d < ROPE_DIM)
    cos = jnp.cos(ang)
    sin = jnp.sin(ang)
    c = jnp.where(lo | hi, cos, 1.0)
    s1 = jnp.where(lo, -sin, 0.0)
    s2 = jnp.where(hi, sin, 0.0)

    def seg(j):
        cols = slice(j * D_MODEL, (j + 1) * D_MODEL)
        return cols, jnp.dot(n, w_ref[:, cols], preferred_element_type=F32)

    def rope(j, scale):
        cols, acc = seg(j)
        for h in range(DA_HEADS):
            sl = acc[:, h * LANES:(h + 1) * LANES]
            r = sl * c + pltpu.roll(sl, LANES - ROPE_HALF, 1) * s1 + pltpu.roll(sl, ROPE_HALF, 1) * s2
            o_ref[:, j * D_MODEL + h * LANES:j * D_MODEL + (h + 1) * LANES] = (r * scale).astype(o_ref.dtype)

    rope(SEG_Q, DA_HEAD_DIM ** -0.5 * math.log2(math.e))
    rope(SEG_K, 1.0)
    for j in (SEG_U, SEG_SV):
        cols, acc = seg(j)
        o_ref[:, cols] = jax.nn.gelu(acc).astype(o_ref.dtype)
    for j in (SEG_GA, SEG_GS):
        cols, acc = seg(j)
        o_ref[:, cols] = jax.nn.sigmoid(acc).astype(o_ref.dtype)
    vt = lax.dot_general(wvt_ref[...], n, (((1,), (1,)), ((), ())), preferred_element_type=F32)
    for h in range(DA_HEADS):
        vt_ref[h, 0] = vt[h * DA_V_DIM:(h + 1) * DA_V_DIM, :].astype(vt_ref.dtype)


def _in_proj(x2, pos, invf, g, w, wvt, B, S, tm):
    T = x2.shape[0]
    nk = S // tm
    const = lambda shape: pl.BlockSpec(shape, lambda i: (0, 0), pipeline_mode=pl.Buffered(1))
    return pl.pallas_call(
        _in_proj_kernel,
        grid=(T // tm,),
        in_specs=[
            pl.BlockSpec((tm, D_MODEL), lambda i: (i, 0)),
            pl.BlockSpec((tm, 1), lambda i: (i, 0)),
            const((1, LANES)),
            const((1, D_MODEL)),
            const((D_MODEL, N_SEG * D_MODEL)),
            const((D_MODEL, D_MODEL)),
        ],
        out_specs=[
            pl.BlockSpec((tm, N_SEG * D_MODEL), lambda i: (i, 0)),
            pl.BlockSpec((DA_HEADS, 1, DA_V_DIM, tm), lambda i: (i // nk, i % nk, 0, 0)),
        ],
        out_shape=[
            jax.ShapeDtypeStruct((T, N_SEG * D_MODEL), BF16),
            jax.ShapeDtypeStruct((B * DA_HEADS, nk, DA_V_DIM, tm), BF16),
        ],
        compiler_params=_params(("arbitrary",)),
        name="in_proj",
    )(x2, pos, invf, g, w, wvt)


def _attn_kernel(lq1_ref, lk1_ref, lq2_ref, lk2_ref, gsub_ref, q_ref, k_ref, vt_ref, o_ref,
                 sa_scr, sb_scr, mca_scr, mcb_scr, m_scr, l_scr, acc_scr, *, bq, lam_init):
    qi = pl.program_id(2)
    q = q_ref[...]
    lane = lax.broadcasted_iota(jnp.int32, (1, LANES), 1)
    zero = jnp.zeros_like(q)
    qs = jnp.concatenate([jnp.where(lane < DA_HEAD_DIM, q, zero),
                          jnp.where(lane >= DA_HEAD_DIM, q, zero)], axis=0)

    m_scr[...] = jnp.full(m_scr.shape, -jnp.inf, F32)
    l_scr[...] = jnp.zeros(l_scr.shape, F32)
    acc_scr[...] = jnp.zeros(acc_scr.shape, F32)

    def scores(kb, buf, masked):
        s_ref, mc_ref = buf
        start = pl.multiple_of(kb * bq, bq)
        k = k_ref[pl.ds(start, bq), :]
        st = lax.dot_general(k, qs, (((1,), (1,)), ((), ())), preferred_element_type=F32)
        if masked:
            key = lax.broadcasted_iota(jnp.int32, (bq, 2 * bq), 0)
            qry = lax.broadcasted_iota(jnp.int32, (bq, 2 * bq), 1) % bq
            st = jnp.where(key // CHUNK <= qry // CHUNK, st, -jnp.inf)
        s_ref[...] = st
        mc_ref[...] = jnp.max(st, axis=0, keepdims=True)

    def accumulate(kb, buf):
        s_ref, mc_ref = buf
        m_prev = m_scr[...]
        m_new = jnp.maximum(m_prev, mc_ref[...])
        alpha = jnp.exp2(m_prev - m_new)
        p = jnp.exp2(s_ref[...] - m_new)
        l_scr[...] = alpha * l_scr[...] + jnp.sum(p, axis=0, keepdims=True)
        pb = p.astype(BF16)
        vb = vt_ref.shape[-1]
        pv = sum(jnp.dot(vt_ref[0, kb * (bq // vb) + i], pb[i * vb:(i + 1) * vb],
                         preferred_element_type=F32) for i in range(bq // vb))
        acc_scr[...] = alpha * acc_scr[...] + pv
        m_scr[...] = m_new

    buf_a, buf_b = (sa_scr, mca_scr), (sb_scr, mcb_scr)
    scores(qi, buf_a, True)

    def pair(u, carry):
        t = 2 * u
        scores(t, buf_b, False)
        accumulate(jnp.where(t == 0, qi, t - 1), buf_a)
        scores(t + 1, buf_a, False)
        accumulate(t, buf_b)
        return carry

    lax.fori_loop(0, qi // 2, pair, 0)

    @pl.when(qi % 2 == 1)
    def _():
        t = qi - 1
        scores(t, buf_b, False)
        accumulate(jnp.where(t == 0, qi, t - 1), buf_a)
        accumulate(t, buf_b)

    @pl.when(qi % 2 == 0)
    def _():
        accumulate(jnp.where(qi == 0, qi, qi - 1), buf_a)

    lam = (jnp.exp(jnp.sum(lq1_ref[...] * lk1_ref[...], keepdims=True))
           - jnp.exp(jnp.sum(lq2_ref[...] * lk2_ref[...], keepdims=True)) + lam_init)
    ot = acc_scr[...] / l_scr[...]
    at = ot[:, :bq] - lam * ot[:, bq:]
    ms = jnp.mean(at * at, axis=0, keepdims=True)
    at = at * lax.rsqrt(ms + EPS) * gsub_ref[...] * (1.0 - lam_init)
    o_ref[...] = at.T.astype(o_ref.dtype)


def _attn(proj, vt, lq1, lk1, lq2, lk2, gsub, B, S, bq, lam_init):
    T = B * S
    nq = S // bq
    small = lambda shape: pl.BlockSpec(shape, lambda b, h, i: (0, 0))
    return pl.pallas_call(
        functools.partial(_attn_kernel, bq=bq, lam_init=lam_init),
        grid=(B, DA_HEADS, nq),
        in_specs=[
            small((1, DA_HEAD_DIM)), small((1, DA_HEAD_DIM)), small((1, DA_HEAD_DIM)), small((1, DA_HEAD_DIM)),
            small((DA_V_DIM, 1)),
            pl.BlockSpec((bq, LANES), lambda b, h, i: (b * nq + i, SEG_Q * DA_HEADS + h)),
            pl.BlockSpec((S, LANES), lambda b, h, i: (b, SEG_K * DA_HEADS + h)),
            pl.BlockSpec((1,) + vt.shape[1:], lambda b, h, i: (b * DA_HEADS + h, 0, 0, 0)),
        ],
        out_specs=pl.BlockSpec((bq, LANES), lambda b, h, i: (b * nq + i, h)),
        out_shape=jax.ShapeDtypeStruct((T, D_MODEL), BF16),
        scratch_shapes=[
            pltpu.VMEM((bq, 2 * bq), F32),
            pltpu.VMEM((bq, 2 * bq), F32),
            pltpu.VMEM((1, 2 * bq), F32),
            pltpu.VMEM((1, 2 * bq), F32),
            pltpu.VMEM((1, 2 * bq), F32),
            pltpu.VMEM((1, 2 * bq), F32),
            pltpu.VMEM((DA_V_DIM, 2 * bq), F32),
        ],
        compiler_params=_params(("arbitrary", "arbitrary", "arbitrary")),
        name="attn",
    )(lq1, lk1, lq2, lk2, gsub, proj, proj, vt)


def _sgate_kernel(u_ref, v_ref, g_ref, b_ref, w_ref, bias_ref, o_ref, *, tm):
    v = v_ref[...].astype(F32)
    mu = jnp.mean(v, axis=-1, keepdims=True)
    vc = v - mu
    var = jnp.mean(vc * vc, axis=-1, keepdims=True)
    vn = (vc * lax.rsqrt(var + LN_EPS) * g_ref[...] + b_ref[...]).astype(BF16)
    i = lax.broadcasted_iota(jnp.int32, (SG_WINDOW, SG_WINDOW), 0) // CHUNK
    j = lax.broadcasted_iota(jnp.int32, (SG_WINDOW, SG_WINDOW), 1) // CHUNK
    keep = j <= i
    for g in range(SG_GROUPS):
        wm = jnp.where(keep, w_ref[g], 0.0).astype(BF16)
        cols = slice(g * SG_GROUP_DIM, (g + 1) * SG_GROUP_DIM)
        for w in range(tm // SG_WINDOW):
            rows = slice(w * SG_WINDOW, (w + 1) * SG_WINDOW)
            mixed = jnp.dot(wm, vn[rows, cols], preferred_element_type=F32) + bias_ref[:, cols]
            o_ref[rows, cols] = (u_ref[rows, cols].astype(F32) * mixed).astype(o_ref.dtype)


def _sgate(proj, ln_g, ln_b, w_s, bias, tm):
    T = proj.shape[0]
    return pl.pallas_call(
        functools.partial(_sgate_kernel, tm=tm),
        grid=(T // tm,),
        in_specs=[
            pl.BlockSpec((tm, D_MODEL), lambda i: (i, SEG_U)),
            pl.BlockSpec((tm, D_MODEL), lambda i: (i, SEG_SV)),
            pl.BlockSpec((1, D_MODEL), lambda i: (0, 0)),
            pl.BlockSpec((1, D_MODEL), lambda i: (0, 0)),
            pl.BlockSpec((SG_GROUPS, SG_WINDOW, SG_WINDOW), lambda i: (0, 0, 0)),
            pl.BlockSpec((SG_WINDOW, D_MODEL), lambda i: (0, 0)),
        ],
        out_specs=pl.BlockSpec((tm, D_MODEL), lambda i: (i, 0)),
        out_shape=jax.ShapeDtypeStruct((T, D_MODEL), BF16),
        compiler_params=_params(("arbitrary",)),
        name="sgate",
    )(proj, proj, ln_g, ln_b, w_s, bias)


def _merge_kernel(x_ref, a_ref, s_ref, ga_ref, gs_ref, wa_ref, ws_ref, wo_ref, o_ref):
    ba = jnp.dot(a_ref[...], wa_ref[...], preferred_element_type=F32)
    bs = jnp.dot(s_ref[...], ws_ref[...], preferred_element_type=F32)
    merged = ga_ref[...].astype(F32) * ba + gs_ref[...].astype(F32) * bs
    o_ref[...] = x_ref[...] + jnp.dot(merged.astype(BF16), wo_ref[...], preferred_element_type=F32)


def _merge(x2, a, sgo, proj, wa, ws, wo, tm):
    T = x2.shape[0]
    row = lambda c: pl.BlockSpec((tm, D_MODEL), lambda i: (i, c))
    wspec = pl.BlockSpec((D_MODEL, D_MODEL), lambda i: (0, 0))
    return pl.pallas_call(
        _merge_kernel,
        grid=(T // tm,),
        in_specs=[row(0), row(0), row(0), row(SEG_GA), row(SEG_GS), wspec, wspec, wspec],
        out_specs=row(0),
        out_shape=jax.ShapeDtypeStruct((T, D_MODEL), F32),
        compiler_params=_params(("arbitrary",)),
        name="merge",
    )(x2, a, sgo, proj, proj, wa, ws, wo)


def _memkv_kernel(m_ref, g_ref, w_ref, o_ref):
    mn = _rms(m_ref[...], g_ref[...]).astype(BF16)
    o_ref[...] = jnp.dot(mn, w_ref[...], preferred_element_type=F32).astype(o_ref.dtype)


def _memkv(mem2, g, w, L):
    R = mem2.shape[0]
    return pl.pallas_call(
        _memkv_kernel,
        grid=(R // L,),
        in_specs=[
            pl.BlockSpec((L, D_MODEL), lambda i: (i, 0)),
            pl.BlockSpec((1, D_MODEL), lambda i: (0, 0)),
            pl.BlockSpec((D_MODEL, 2 * D_MODEL), lambda i: (0, 0)),
        ],
        out_specs=pl.BlockSpec((L, 2 * D_MODEL), lambda i: (i, 0)),
        out_shape=jax.ShapeDtypeStruct((R, 2 * D_MODEL), BF16),
        compiler_params=_params(("arbitrary",)),
        name="memkv",
    )(mem2, g, w)


def _xattn_kernel(h_ref, g_ref, wq_ref, k_ref, v_ref, wo_ref, o_ref):
    h = h_ref[...]
    hn = _rms(h, g_ref[...]).astype(BF16)
    q = (jnp.dot(hn, wq_ref[...], preferred_element_type=F32) * XA_HEAD_DIM ** -0.5).astype(BF16)
    outs = []
    for hd in range(XA_HEADS):
        cols = slice(hd * XA_HEAD_DIM, (hd + 1) * XA_HEAD_DIM)
        s = lax.dot_general(q[:, cols], k_ref[:, cols], (((1,), (1,)), ((), ())), preferred_element_type=F32)
        p = jnp.exp(s - jnp.max(s, axis=-1, keepdims=True))
        l = jnp.sum(p, axis=-1, keepdims=True)
        pv = jnp.dot(p.astype(BF16), v_ref[:, cols], preferred_element_type=F32)
        outs.append((pv / l).astype(BF16))
    o = jnp.concatenate(outs, axis=-1)
    o_ref[...] = h + jnp.dot(o, wo_ref[...], preferred_element_type=F32)


def _xattn(h1, g, wq, kv, wo, S, L, tm):
    T = h1.shape[0]
    per_b = S // tm
    wspec = pl.BlockSpec((D_MODEL, D_MODEL), lambda i: (0, 0))
    return pl.pallas_call(
        _xattn_kernel,
        grid=(T // tm,),
        in_specs=[
            pl.BlockSpec((tm, D_MODEL), lambda i: (i, 0)),
            pl.BlockSpec((1, D_MODEL), lambda i: (0, 0)),
            wspec,
            pl.BlockSpec((L, D_MODEL), lambda i: (i // per_b, 0)),
            pl.BlockSpec((L, D_MODEL), lambda i: (i // per_b, 1)),
            wspec,
        ],
        out_specs=pl.BlockSpec((tm, D_MODEL), lambda i: (i, 0)),
        out_shape=jax.ShapeDtypeStruct((T, D_MODEL), F32),
        compiler_params=_params(("arbitrary",)),
        name="xattn",
    )(h1, g, wq, kv, kv, wo)


def _ffn_kernel(h_ref, g_ref, w1_ref, w2_ref, gf_ref, o_ref, *, tf, final):
    h = h_ref[...]
    n = _rms(h, g_ref[...]).astype(BF16)
    out = h
    for j in range(D_FF // tf):
        f = jnp.maximum(jnp.dot(n, w1_ref[:, j * tf:(j + 1) * tf], preferred_element_type=F32), 0.0)
        out = out + jnp.dot((f * f).astype(BF16), w2_ref[j * tf:(j + 1) * tf, :], preferred_element_type=F32)
    o_ref[...] = _rms(out, gf_ref[...]) if final else out


def _ffn(h2, g, w1, w2, gf, tm, tf, final):
    T = h2.shape[0]
    const = lambda shape: pl.BlockSpec(shape, lambda i: (0, 0), pipeline_mode=pl.Buffered(1))
    return pl.pallas_call(
        functools.partial(_ffn_kernel, tf=tf, final=final),
        grid=(T // tm,),
        in_specs=[
            pl.BlockSpec((tm, D_MODEL), lambda i: (i, 0)),
            const((1, D_MODEL)),
            const((D_MODEL, D_FF)),
            const((D_FF, D_MODEL)),
            const((1, D_MODEL)),
        ],
        out_specs=pl.BlockSpec((tm, D_MODEL), lambda i: (i, 0)),
        out_shape=jax.ShapeDtypeStruct((T, D_MODEL), F32),
        compiler_params=_params(("arbitrary",)),
        name="ffn",
    )(h2, g, w1, w2, gf)


def _tile(n, pref):
    t = min(n, pref)
    assert n % t == 0, (n, t)
    return t


def kernel(x, mem, positions, g_mix, w_in, lam_q1, lam_k1, lam_q2, lam_k2, g_subln, sg_ln_g, sg_ln_b, sg_w, sg_b, w_branch_attn, w_branch_sg, w_out, g_xa, g_mem, w_xq, w_xkv, w_xo, g_ffn, w_ff1, w_ff2, g_final):
    B, S, _ = x.shape
    L = mem.shape[1]
    depth = w_in.shape[0]
    T = B * S
    tm = _tile(S, 512)
    bq = _tile(S, 1024)

    idx = jnp.arange(0, ROPE_DIM, 2, dtype=F32)
    inv_freq = jnp.power(jnp.float32(ROPE_THETA), -idx / ROPE_DIM)
    d = jnp.arange(LANES) % DA_HEAD_DIM
    invf = jnp.where(d < ROPE_DIM, inv_freq[d % ROPE_HALF], 0.0).reshape(1, LANES)
    pos = positions.reshape(T, 1).astype(F32)

    row = lambda p: p.reshape(1, -1).astype(F32)
    h = x.reshape(T, D_MODEL)
    mem2 = mem.reshape(B * L, D_MODEL)
    for l in range(depth):
        lam_init = 0.8 - 0.6 * math.exp(-0.3 * l)
        w = w_in[l].astype(BF16)
        w_rows = jnp.concatenate([w[:, :2 * D_MODEL], w[:, 3 * D_MODEL:]], axis=1)
        w_vt = w[:, 2 * D_MODEL:3 * D_MODEL].T
        proj, vt = _in_proj(h, pos, invf, row(g_mix[l]), w_rows, w_vt, B, S, tm)
        a = _attn(proj, vt, row(lam_q1[l]), row(lam_k1[l]), row(lam_q2[l]), row(lam_k2[l]),
                  g_subln[l].reshape(DA_V_DIM, 1).astype(F32), B, S, bq, lam_init)
        bias = jnp.repeat(sg_b[l].T.astype(F32), SG_GROUP_DIM, axis=1)
        sgo = _sgate(proj, row(sg_ln_g[l]), row(sg_ln_b[l]), sg_w[l].astype(F32), bias, tm)
        h1 = _merge(h, a, sgo, proj, w_branch_attn[l].astype(BF16), w_branch_sg[l].astype(BF16),
                    w_out[l].astype(BF16), tm)
        kv = _memkv(mem2, row(g_mem[l]), w_xkv[l].astype(BF16), L)
        h2 = _xattn(h1, row(g_xa[l]), w_xq[l].astype(BF16), kv, w_xo[l].astype(BF16), S, L, tm)
        h = _ffn(h2, row(g_ffn[l]), w_ff1[l].astype(BF16), w_ff2[l].astype(BF16), row(g_final),
                 tm, _tile(D_FF, 1024), final=(l == depth - 1))
    return h.reshape(B, S, D_MODEL)
```

```python
import functools
import math

import jax
import jax.numpy as jnp
from jax import lax
from jax.experimental import pallas as pl
from jax.experimental.pallas import tpu as pltpu

F32 = jnp.float32
BF16 = jnp.bfloat16

D_MODEL = 1024
CHUNK = 64
EPS = 1e-6
LN_EPS = 1e-5
DA_HEADS = 8
DA_HEAD_DIM = 64
DA_V_DIM = 128
ROPE_THETA = 500000.0
ROPE_DIM = 16
ROPE_HALF = ROPE_DIM // 2
SG_GROUPS = 8
SG_GROUP_DIM = 128
SG_WINDOW = 128
XA_HEADS = 4
XA_HEAD_DIM = 256
D_FF = 4096
N_SEG = 6
SEG_Q, SEG_K, SEG_U, SEG_SV, SEG_GA, SEG_GS = range(N_SEG)

LANES = 128
VMEM_LIMIT = 56 * 1024 * 1024


def _rms(x, g):
    return x * lax.rsqrt(jnp.mean(x * x, axis=-1, keepdims=True) + EPS) * g


def _params(sem, vmem=VMEM_LIMIT, flags=None):
    return pltpu.CompilerParams(dimension_semantics=sem, vmem_limit_bytes=vmem, flags=flags)


def _in_proj_kernel(x_ref, pos_ref, invf_ref, g_ref, w_ref, wvt_ref, o_ref, vt_ref):
    n = _rms(x_ref[...], g_ref[...]).astype(BF16)
    ang = pos_ref[...] * invf_ref[...]
    d = lax.broadcasted_iota(jnp.int32, (1, LANES), 1) % DA_HEAD_DIM
    lo = d < ROPE_HALF
    hi = (d >= ROPE_HALF) & (d < ROPE_DIM)
    cos = jnp.cos(ang)
    sin = jnp.sin(ang)
    c = jnp.where(lo | hi, cos, 1.0)
    s1 = jnp.where(lo, -sin, 0.0)
    s2 = jnp.where(hi, sin, 0.0)

    def seg(j):
        cols = slice(j * D_MODEL, (j + 1) * D_MODEL)
        jw = j + (j > SEG_K)
        return cols, jnp.dot(n, w_ref[:, jw * D_MODEL:(jw + 1) * D_MODEL], preferred_element_type=F32)

    def rope(j, scale):
        cols, acc = seg(j)
        for h in range(DA_HEADS):
            sl = acc[:, h * LANES:(h + 1) * LANES]
            r = sl * c + pltpu.roll(sl, LANES - ROPE_HALF, 1) * s1 + pltpu.roll(sl, ROPE_HALF, 1) * s2
            o_ref[:, j * D_MODEL + h * LANES:j * D_MODEL + (h + 1) * LANES] = (r * scale).astype(o_ref.dtype)

    rope(SEG_Q, DA_HEAD_DIM ** -0.5 * math.log2(math.e))
    rope(SEG_K, 1.0)
    for j in (SEG_U, SEG_SV):
        cols, acc = seg(j)
        o_ref[:, cols] = jax.nn.gelu(acc).astype(o_ref.dtype)
    for j in (SEG_GA, SEG_GS):
        cols, acc = seg(j)
        o_ref[:, cols] = jax.nn.sigmoid(acc).astype(o_ref.dtype)
    vt = lax.dot_general(wvt_ref[...], n, (((1,), (1,)), ((), ())), preferred_element_type=F32)
    for h in range(DA_HEADS):
        vt_ref[h, 0] = vt[h * DA_V_DIM:(h + 1) * DA_V_DIM, :].astype(vt_ref.dtype)


def _in_proj(x2, pos, invf, g, w, wvt, B, S, tm):
    T = x2.shape[0]
    nk = S // tm
    const = lambda shape: pl.BlockSpec(shape, lambda i: (0, 0), pipeline_mode=pl.Buffered(1))
    return pl.pallas_call(
        _in_proj_kernel,
        grid=(T // tm,),
        in_specs=[
            pl.BlockSpec((tm, D_MODEL), lambda i: (i, 0)),
            pl.BlockSpec((tm, 1), lambda i: (i, 0)),
            const((1, LANES)),
            const((1, D_MODEL)),
            const((D_MODEL, (N_SEG + 1) * D_MODEL)),
            const((D_MODEL, D_MODEL)),
        ],
        out_specs=[
            pl.BlockSpec((tm, N_SEG * D_MODEL), lambda i: (i, 0)),
            pl.BlockSpec((DA_HEADS, 1, DA_V_DIM, tm), lambda i: (i // nk, i % nk, 0, 0)),
        ],
        out_shape=[
            jax.ShapeDtypeStruct((T, N_SEG * D_MODEL), BF16),
            jax.ShapeDtypeStruct((B * DA_HEADS, nk, DA_V_DIM, tm), BF16),
        ],
        compiler_params=_params(("arbitrary",)),
        name="in_proj",
    )(x2, pos, invf, g, w, wvt)


def _attn_kernel(lq1_ref, lk1_ref, lq2_ref, lk2_ref, gsub_ref, q_ref, k_ref, vt_ref, o_ref,
                 sa_scr, sb_scr, mca_scr, mcb_scr, m_scr, l_scr, acc_scr, *, bq, lam_init):
    qi = pl.program_id(2)
    q = q_ref[...]
    lane = lax.broadcasted_iota(jnp.int32, (1, LANES), 1)
    zero = jnp.zeros_like(q)
    qs = jnp.concatenate([jnp.where(lane < DA_HEAD_DIM, q, zero),
                          jnp.where(lane >= DA_HEAD_DIM, q, zero)], axis=0)

    m_scr[...] = jnp.full(m_scr.shape, -jnp.inf, F32)
    l_scr[...] = jnp.zeros(l_scr.shape, F32)
    acc_scr[...] = jnp.zeros(acc_scr.shape, F32)

    def scores(kb, buf, masked):
        s_ref, mc_ref = buf
        k = k_ref[pl.ds(pl.multiple_of(kb * bq, bq), bq), :]
        st = lax.dot_general(k, qs, (((1,), (1,)), ((), ())), preferred_element_type=F32)
        if masked:
            key = lax.broadcasted_iota(jnp.int32, (bq, 2 * bq), 0)
            qry = lax.broadcasted_iota(jnp.int32, (bq, 2 * bq), 1) % bq
            st = jnp.where(key // CHUNK <= qry // CHUNK, st, -jnp.inf)
        s_ref[...] = st
        mc_ref[...] = jnp.max(st, axis=0, keepdims=True)

    def accumulate(kb, buf):
        s_ref, mc_ref = buf
        m_prev = m_scr[...]
        m_new = jnp.maximum(m_prev, mc_ref[...])
        alpha = jnp.exp2(m_prev - m_new)
        p = jnp.exp2(s_ref[...] - m_new)
        l_scr[...] = alpha * l_scr[...] + jnp.sum(p, axis=0, keepdims=True)
        pb = p.astype(BF16)
        vb = vt_ref.shape[-1]
        pv = sum(jnp.dot(vt_ref[0, kb * (bq // vb) + i], pb[i * vb:(i + 1) * vb],
                         preferred_element_type=F32) for i in range(bq // vb))
        acc_scr[...] = alpha * acc_scr[...] + pv
        m_scr[...] = m_new

    buf_a, buf_b = (sa_scr, mca_scr), (sb_scr, mcb_scr)
    scores(qi, buf_a, True)

    def pair(u, carry):
        t = 2 * u
        scores(t, buf_b, False)
        accumulate(jnp.where(t == 0, qi, t - 1), buf_a)
        scores(t + 1, buf_a, False)
        accumulate(t, buf_b)
        return carry

    lax.fori_loop(0, qi // 2, pair, 0)

    @pl.when(qi % 2 == 1)
    def _():
        t = qi - 1
        scores(t, buf_b, False)
        accumulate(jnp.where(t == 0, qi, t - 1), buf_a)
        accumulate(t, buf_b)

    @pl.when(qi % 2 == 0)
    def _():
        accumulate(jnp.where(qi == 0, qi, qi - 1), buf_a)

    lam = (jnp.exp(jnp.sum(lq1_ref[...] * lk1_ref[...], keepdims=True))
           - jnp.exp(jnp.sum(lq2_ref[...] * lk2_ref[...], keepdims=True)) + lam_init)
    ot = acc_scr[...] / l_scr[...]
    at = ot[:, :bq] - lam * ot[:, bq:]
    ms = jnp.mean(at * at, axis=0, keepdims=True)
    at = at * lax.rsqrt(ms + EPS) * gsub_ref[...] * (1.0 - lam_init)
    o_ref[...] = at.T.astype(o_ref.dtype)


def _attn(proj, vt, lq1, lk1, lq2, lk2, gsub, B, S, bq, lam_init):
    T = B * S
    nq = S // bq
    small = lambda shape: pl.BlockSpec(shape, lambda b, h, i: (0, 0))
    return pl.pallas_call(
        functools.partial(_attn_kernel, bq=bq, lam_init=lam_init),
        grid=(B, DA_HEADS, nq),
        in_specs=[
            small((1, DA_HEAD_DIM)), small((1, DA_HEAD_DIM)), small((1, DA_HEAD_DIM)), small((1, DA_HEAD_DIM)),
            small((DA_V_DIM, 1)),
            pl.BlockSpec((bq, LANES), lambda b, h, i: (b * nq + i, SEG_Q * DA_HEADS + h)),
            pl.BlockSpec((S, LANES), lambda b, h, i: (b, SEG_K * DA_HEADS + h)),
            pl.BlockSpec((1,) + vt.shape[1:], lambda b, h, i: (b * DA_HEADS + h, 0, 0, 0)),
        ],
        out_specs=pl.BlockSpec((bq, LANES), lambda b, h, i: (b * nq + i, h)),
        out_shape=jax.ShapeDtypeStruct((T, D_MODEL), BF16),
        scratch_shapes=[
            pltpu.VMEM((bq, 2 * bq), F32),
            pltpu.VMEM((bq, 2 * bq), F32),
            pltpu.VMEM((1, 2 * bq), F32),
            pltpu.VMEM((1, 2 * bq), F32),
            pltpu.VMEM((1, 2 * bq), F32),
            pltpu.VMEM((1, 2 * bq), F32),
            pltpu.VMEM((DA_V_DIM, 2 * bq), F32),
        ],
        compiler_params=_params(("arbitrary", "arbitrary", "arbitrary")),
        name="attn",
    )(lq1, lk1, lq2, lk2, gsub, proj, proj, vt)


def _sgate_kernel(u_ref, v_ref, g_ref, b_ref, w_ref, bias_ref, o_ref, *, tm):
    v = v_ref[...].astype(F32)
    mu = jnp.mean(v, axis=-1, keepdims=True)
    vc = v - mu
    var = jnp.mean(vc * vc, axis=-1, keepdims=True)
    vn = (vc * lax.rsqrt(var + LN_EPS) * g_ref[...] + b_ref[...]).astype(BF16)
    i = lax.broadcasted_iota(jnp.int32, (SG_WINDOW, SG_WINDOW), 0) // CHUNK
    j = lax.broadcasted_iota(jnp.int32, (SG_WINDOW, SG_WINDOW), 1) // CHUNK
    keep = j <= i
    for g in range(SG_GROUPS):
        wm = jnp.where(keep, w_ref[g], 0.0).astype(BF16)
        cols = slice(g * SG_GROUP_DIM, (g + 1) * SG_GROUP_DIM)
        for w in range(tm // SG_WINDOW):
            rows = slice(w * SG_WINDOW, (w + 1) * SG_WINDOW)
            mixed = jnp.dot(wm, vn[rows, cols], preferred_element_type=F32) + bias_ref[:, cols]
            o_ref[rows, cols] = (u_ref[rows, cols].astype(F32) * mixed).astype(o_ref.dtype)


def _merge_value(x_ref, a_ref, s_ref, ga_ref, gs_ref, wa_ref, ws_ref, wo_ref):
    ba = jnp.dot(a_ref[...], wa_ref[...], preferred_element_type=F32)
    bs = jnp.dot(s_ref[...], ws_ref[...], preferred_element_type=F32)
    merged = ga_ref[...].astype(F32) * ba + gs_ref[...].astype(F32) * bs
    return x_ref[...] + jnp.dot(merged.astype(BF16), wo_ref[...], preferred_element_type=F32)


def _memkv_kernel(m_ref, g_ref, w_ref, o_ref):
    mn = _rms(m_ref[...], g_ref[...]).astype(BF16)
    o_ref[...] = jnp.dot(mn, w_ref[...], preferred_element_type=F32).astype(o_ref.dtype)


def _memkv(mem2, g, w, L):
    R = mem2.shape[0]
    return pl.pallas_call(
        _memkv_kernel,
        grid=(R // L,),
        in_specs=[
            pl.BlockSpec((L, D_MODEL), lambda i: (i, 0)),
            pl.BlockSpec((1, D_MODEL), lambda i: (0, 0)),
            pl.BlockSpec((D_MODEL, 2 * D_MODEL), lambda i: (0, 0)),
        ],
        out_specs=pl.BlockSpec((L, 2 * D_MODEL), lambda i: (i, 0)),
        out_shape=jax.ShapeDtypeStruct((R, 2 * D_MODEL), BF16),
        compiler_params=_params(("arbitrary",)),
        name="memkv",
    )(mem2, g, w)


def _xattn_value(h, g_ref, wq_ref, k_ref, v_ref, wo_ref):
    hn = _rms(h, g_ref[...]).astype(BF16)
    q = (jnp.dot(hn, wq_ref[...], preferred_element_type=F32) * XA_HEAD_DIM ** -0.5).astype(BF16)
    outs = []
    for hd in range(XA_HEADS):
        cols = slice(hd * XA_HEAD_DIM, (hd + 1) * XA_HEAD_DIM)
        s = lax.dot_general(q[:, cols], k_ref[:, cols], (((1,), (1,)), ((), ())), preferred_element_type=F32)
        p = jnp.exp(s - jnp.max(s, axis=-1, keepdims=True))
        l = jnp.sum(p, axis=-1, keepdims=True)
        pv = jnp.dot(p.astype(BF16), v_ref[:, cols], preferred_element_type=F32)
        outs.append((pv / l).astype(BF16))
    o = jnp.concatenate(outs, axis=-1)
    return h + jnp.dot(o, wo_ref[...], preferred_element_type=F32)


def _mix_kernel(x_ref, a_ref, u_ref, sv_ref, ga_ref, gs_ref, lng_ref, lnb_ref, wsg_ref, bias_ref,
                wa_ref, ws_ref, wo_ref, gxa_ref, wq_ref, k_ref, v_ref, wxo_ref, o_ref, sgo_scr, *, tm):
    _sgate_kernel(u_ref, sv_ref, lng_ref, lnb_ref, wsg_ref, bias_ref, sgo_scr, tm=tm)
    h1 = _merge_value(x_ref, a_ref, sgo_scr, ga_ref, gs_ref, wa_ref, ws_ref, wo_ref)
    o_ref[...] = _xattn_value(h1, gxa_ref, wq_ref, k_ref, v_ref, wxo_ref)


def _mix(x2, a, proj, ln_g, ln_b, w_s, bias, wa, ws, wo, g_xa, wq, kv, wxo, S, L, tm):
    T = x2.shape[0]
    per_b = S // tm
    row = lambda c: pl.BlockSpec((tm, D_MODEL), lambda i: (i, c))
    const = lambda shape: pl.BlockSpec(shape, lambda i: (0,) * len(shape), pipeline_mode=pl.Buffered(1))
    wspec = const((D_MODEL, D_MODEL))
    return pl.pallas_call(
        functools.partial(_mix_kernel, tm=tm),
        grid=(T // tm,),
        in_specs=[
            row(0), row(0), row(SEG_U), row(SEG_SV), row(SEG_GA), row(SEG_GS),
            const((1, D_MODEL)), const((1, D_MODEL)),
            const((SG_GROUPS, SG_WINDOW, SG_WINDOW)), const((SG_WINDOW, D_MODEL)),
            wspec, wspec, wspec,
            const((1, D_MODEL)), wspec,
            pl.BlockSpec((L, D_MODEL), lambda i: (i // per_b, 0)),
            pl.BlockSpec((L, D_MODEL), lambda i: (i // per_b, 1)),
            wspec,
        ],
        out_specs=row(0),
        out_shape=jax.ShapeDtypeStruct((T, D_MODEL), F32),
        scratch_shapes=[pltpu.VMEM((tm, D_MODEL), BF16)],
        compiler_params=_params(("arbitrary",)),
        name="mix",
    )(x2, a, proj, proj, proj, proj, ln_g, ln_b, w_s, bias, wa, ws, wo, g_xa, wq, kv, kv, wxo)


def _ffn_kernel(h_ref, g_ref, w1_ref, w2_ref, gf_ref, o_ref, *, tf, final):
    h = h_ref[...]
    n = _rms(h, g_ref[...]).astype(BF16)
    out = h
    for j in range(D_FF // tf):
        f = jnp.maximum(jnp.dot(n, w1_ref[:, j * tf:(j + 1) * tf], preferred_element_type=F32), 0.0)
        out = out + jnp.dot((f * f).astype(BF16), w2_ref[j * tf:(j + 1) * tf, :], preferred_element_type=F32)
    o_ref[...] = _rms(out, gf_ref[...]) if final else out


def _ffn(h2, g, w1, w2, gf, tm, tf, final):
    T = h2.shape[0]
    const = lambda shape: pl.BlockSpec(shape, lambda i: (0, 0), pipeline_mode=pl.Buffered(1))
    return pl.pallas_call(
        functools.partial(_ffn_kernel, tf=tf, final=final),
        grid=(T // tm,),
        in_specs=[
            pl.BlockSpec((tm, D_MODEL), lambda i: (i, 0)),
            const((1, D_MODEL)),
            const((D_MODEL, D_FF)),
            const((D_FF, D_MODEL)),
            const((1, D_MODEL)),
        ],
        out_specs=pl.BlockSpec((tm, D_MODEL), lambda i: (i, 0)),
        out_shape=jax.ShapeDtypeStruct((T, D_MODEL), F32),
        compiler_params=_params(("arbitrary",)),
        name="ffn",
    )(h2, g, w1, w2, gf)


def _tile(n, pref):
    t = min(n, pref)
    assert n % t == 0, (n, t)
    return t


def kernel(x, mem, positions, g_mix, w_in, lam_q1, lam_k1, lam_q2, lam_k2, g_subln, sg_ln_g, sg_ln_b, sg_w, sg_b, w_branch_attn, w_branch_sg, w_out, g_xa, g_mem, w_xq, w_xkv, w_xo, g_ffn, w_ff1, w_ff2, g_final):
    B, S, _ = x.shape
    L = mem.shape[1]
    depth = w_in.shape[0]
    T = B * S
    tm = _tile(S, 512)
    bq = _tile(S, 1024)

    idx = jnp.arange(0, ROPE_DIM, 2, dtype=F32)
    inv_freq = jnp.power(jnp.float32(ROPE_THETA), -idx / ROPE_DIM)
    d = jnp.arange(LANES) % DA_HEAD_DIM
    invf = jnp.where(d < ROPE_DIM, inv_freq[d % ROPE_HALF], 0.0).reshape(1, LANES)
    pos = positions.reshape(T, 1).astype(F32)

    row = lambda p: p.reshape(1, -1).astype(F32)
    h = x.reshape(T, D_MODEL)
    mem2 = mem.reshape(B * L, D_MODEL)
    for l in range(depth):
        lam_init = 0.8 - 0.6 * math.exp(-0.3 * l)
        w = w_in[l].astype(BF16)
        w_vt = w[:, 2 * D_MODEL:3 * D_MODEL].T
        proj, vt = _in_proj(h, pos, invf, row(g_mix[l]), w, w_vt, B, S, tm)
        a = _attn(proj, vt, row(lam_q1[l]), row(lam_k1[l]), row(lam_q2[l]), row(lam_k2[l]),
                  g_subln[l].reshape(DA_V_DIM, 1).astype(F32), B, S, bq, lam_init)
        bias = jnp.repeat(sg_b[l].T.astype(F32), SG_GROUP_DIM, axis=1)
        kv = _memkv(mem2, row(g_mem[l]), w_xkv[l].astype(BF16), L)
        h2 = _mix(h, a, proj, row(sg_ln_g[l]), row(sg_ln_b[l]), sg_w[l].astype(F32), bias,
                  w_branch_attn[l].astype(BF16), w_branch_sg[l].astype(BF16), w_out[l].astype(BF16),
                  row(g_xa[l]), w_xq[l].astype(BF16), kv, w_xo[l].astype(BF16), S, L, tm)
        h = _ffn(h2, row(g_ffn[l]), w_ff1[l].astype(BF16), w_ff2[l].astype(BF16), row(g_final),
                 tm, _tile(D_FF, 1024), final=(l == depth - 1))
    return h.reshape(B, S, D_MODEL)
```

```python
import functools
import math

import jax
import jax.numpy as jnp
from jax import lax
from jax.experimental import pallas as pl
from jax.experimental.pallas import tpu as pltpu

F32 = jnp.float32
BF16 = jnp.bfloat16

D_MODEL = 1024
CHUNK = 64
EPS = 1e-6
LN_EPS = 1e-5
DA_HEADS = 8
DA_HEAD_DIM = 64
DA_V_DIM = 128
ROPE_THETA = 500000.0
ROPE_DIM = 16
ROPE_HALF = ROPE_DIM // 2
SG_GROUPS = 8
SG_GROUP_DIM = 128
SG_WINDOW = 128
XA_HEADS = 4
XA_HEAD_DIM = 256
D_FF = 4096
N_SEG = 6
SEG_Q, SEG_K, SEG_U, SEG_SV, SEG_GA, SEG_GS = range(N_SEG)

LANES = 128
VMEM_LIMIT = 56 * 1024 * 1024


def _rms(x, g):
    return x * lax.rsqrt(jnp.mean(x * x, axis=-1, keepdims=True) + EPS) * g


def _params(sem, vmem=VMEM_LIMIT, flags=None):
    return pltpu.CompilerParams(dimension_semantics=sem, vmem_limit_bytes=vmem, flags=flags)


def _in_proj_kernel(x_ref, pos_ref, invf_ref, g_ref, w_ref, wvt_ref, o_ref, vt_ref):
    n = _rms(x_ref[...], g_ref[...]).astype(BF16)
    ang = pos_ref[...] * invf_ref[...]
    d = lax.broadcasted_iota(jnp.int32, (1, LANES), 1) % DA_HEAD_DIM
    lo = d < ROPE_HALF
    hi = (d >= ROPE_HALF) & (d < ROPE_DIM)
    cos = jnp.cos(ang)
    sin = jnp.sin(ang)
    c = jnp.where(lo | hi, cos, 1.0)
    s1 = jnp.where(lo, -sin, 0.0)
    s2 = jnp.where(hi, sin, 0.0)

    def seg(j):
        cols = slice(j * D_MODEL, (j + 1) * D_MODEL)
        jw = j + (j > SEG_K)
        return cols, jnp.dot(n, w_ref[:, jw * D_MODEL:(jw + 1) * D_MODEL], preferred_element_type=F32)

    def rope(j, scale):
        cols, acc = seg(j)
        for h in range(DA_HEADS):
            sl = acc[:, h * LANES:(h + 1) * LANES]
            r = sl * c + pltpu.roll(sl, LANES - ROPE_HALF, 1) * s1 + pltpu.roll(sl, ROPE_HALF, 1) * s2
            o_ref[:, j * D_MODEL + h * LANES:j * D_MODEL + (h + 1) * LANES] = (r * scale).astype(o_ref.dtype)

    rope(SEG_Q, DA_HEAD_DIM ** -0.5 * math.log2(math.e))
    rope(SEG_K, 1.0)
    for j in (SEG_U, SEG_SV):
        cols, acc = seg(j)
        o_ref[:, cols] = jax.nn.gelu(acc).astype(o_ref.dtype)
    for j in (SEG_GA, SEG_GS):
        cols, acc = seg(j)
        o_ref[:, cols] = jax.nn.sigmoid(acc).astype(o_ref.dtype)
    vt = lax.dot_general(wvt_ref[...], n, (((1,), (1,)), ((), ())), preferred_element_type=F32)
    for h in range(DA_HEADS):
        vt_ref[h, 0] = vt[h * DA_V_DIM:(h + 1) * DA_V_DIM, :].astype(vt_ref.dtype)


def _in_proj(x2, pos, invf, g, w, wvt, B, S, tm):
    T = x2.shape[0]
    nk = S // tm
    const = lambda shape: pl.BlockSpec(shape, lambda i: (0, 0), pipeline_mode=pl.Buffered(1))
    return pl.pallas_call(
        _in_proj_kernel,
        grid=(T // tm,),
        in_specs=[
            pl.BlockSpec((tm, D_MODEL), lambda i: (i, 0)),
            pl.BlockSpec((tm, 1), lambda i: (i, 0)),
            const((1, LANES)),
            const((1, D_MODEL)),
            const((D_MODEL, (N_SEG + 1) * D_MODEL)),
            const((D_MODEL, D_MODEL)),
        ],
        out_specs=[
            pl.BlockSpec((tm, N_SEG * D_MODEL), lambda i: (i, 0)),
            pl.BlockSpec((DA_HEADS, 1, DA_V_DIM, tm), lambda i: (i // nk, i % nk, 0, 0)),
        ],
        out_shape=[
            jax.ShapeDtypeStruct((T, N_SEG * D_MODEL), BF16),
            jax.ShapeDtypeStruct((B * DA_HEADS, nk, DA_V_DIM, tm), BF16),
        ],
        compiler_params=_params(("arbitrary",)),
        name="in_proj",
    )(x2, pos, invf, g, w, wvt)


def _attn_kernel(lq1_ref, lk1_ref, lq2_ref, lk2_ref, gsub_ref, q_ref, k_ref, vt_ref, o_ref,
                 sa_scr, sb_scr, mca_scr, mcb_scr, m_scr, l_scr, acc_scr, *, bq, lam_init):
    qi = pl.program_id(2)
    q = q_ref[...]
    lane = lax.broadcasted_iota(jnp.int32, (1, LANES), 1)
    zero = jnp.zeros_like(q)
    qs = jnp.concatenate([jnp.where(lane < DA_HEAD_DIM, q, zero),
                          jnp.where(lane >= DA_HEAD_DIM, q, zero)], axis=0)

    m_scr[...] = jnp.full(m_scr.shape, -jnp.inf, F32)
    l_scr[...] = jnp.zeros(l_scr.shape, F32)
    acc_scr[...] = jnp.zeros(acc_scr.shape, F32)

    def scores(kb, buf, masked):
        s_ref, mc_ref = buf
        k = k_ref[pl.ds(pl.multiple_of(kb * bq, bq), bq), :]
        st = lax.dot_general(k, qs, (((1,), (1,)), ((), ())), preferred_element_type=F32)
        if masked:
            key = lax.broadcasted_iota(jnp.int32, (bq, 2 * bq), 0)
            qry = lax.broadcasted_iota(jnp.int32, (bq, 2 * bq), 1) % bq
            st = jnp.where(key // CHUNK <= qry // CHUNK, st, -jnp.inf)
        s_ref[...] = st
        mc_ref[...] = jnp.max(st, axis=0, keepdims=True)

    def accumulate(kb, buf):
        s_ref, mc_ref = buf
        m_prev = m_scr[...]
        m_new = jnp.maximum(m_prev, mc_ref[...])
        alpha = jnp.exp2(m_prev - m_new)
        p = jnp.exp2(s_ref[...] - m_new)
        l_scr[...] = alpha * l_scr[...] + jnp.sum(p, axis=0, keepdims=True)
        pb = p.astype(BF16)
        vb = vt_ref.shape[-1]
        pv = sum(jnp.dot(vt_ref[0, kb * (bq // vb) + i], pb[i * vb:(i + 1) * vb],
                         preferred_element_type=F32) for i in range(bq // vb))
        acc_scr[...] = alpha * acc_scr[...] + pv
        m_scr[...] = m_new

    buf_a, buf_b = (sa_scr, mca_scr), (sb_scr, mcb_scr)
    scores(qi, buf_a, True)

    def pair(u, carry):
        t = 2 * u
        scores(t, buf_b, False)
        accumulate(jnp.where(t == 0, qi, t - 1), buf_a)
        scores(t + 1, buf_a, False)
        accumulate(t, buf_b)
        return carry

    lax.fori_loop(0, qi // 2, pair, 0)

    @pl.when(qi % 2 == 1)
    def _():
        t = qi - 1
        scores(t, buf_b, False)
        accumulate(jnp.where(t == 0, qi, t - 1), buf_a)
        accumulate(t, buf_b)

    @pl.when(qi % 2 == 0)
    def _():
        accumulate(jnp.where(qi == 0, qi, qi - 1), buf_a)

    lam = (jnp.exp(jnp.sum(lq1_ref[...] * lk1_ref[...], keepdims=True))
           - jnp.exp(jnp.sum(lq2_ref[...] * lk2_ref[...], keepdims=True)) + lam_init)
    ot = acc_scr[...] / l_scr[...]
    at = ot[:, :bq] - lam * ot[:, bq:]
    ms = jnp.mean(at * at, axis=0, keepdims=True)
    at = at * lax.rsqrt(ms + EPS) * gsub_ref[...] * (1.0 - lam_init)
    o_ref[...] = at.T.astype(o_ref.dtype)


def _attn(proj, vt, lq1, lk1, lq2, lk2, gsub, B, S, bq, lam_init):
    T = B * S
    nq = S // bq
    small = lambda shape: pl.BlockSpec(shape, lambda b, h, i: (0, 0))
    return pl.pallas_call(
        functools.partial(_attn_kernel, bq=bq, lam_init=lam_init),
        grid=(B, DA_HEADS, nq),
        in_specs=[
            small((1, DA_HEAD_DIM)), small((1, DA_HEAD_DIM)), small((1, DA_HEAD_DIM)), small((1, DA_HEAD_DIM)),
            small((DA_V_DIM, 1)),
            pl.BlockSpec((bq, LANES), lambda b, h, i: (b * nq + i, SEG_Q * DA_HEADS + h)),
            pl.BlockSpec((S, LANES), lambda b, h, i: (b, SEG_K * DA_HEADS + h)),
            pl.BlockSpec((1,) + vt.shape[1:], lambda b, h, i: (b * DA_HEADS + h, 0, 0, 0)),
        ],
        out_specs=pl.BlockSpec((bq, LANES), lambda b, h, i: (b * nq + i, h)),
        out_shape=jax.ShapeDtypeStruct((T, D_MODEL), BF16),
        scratch_shapes=[
            pltpu.VMEM((bq, 2 * bq), F32),
            pltpu.VMEM((bq, 2 * bq), F32),
            pltpu.VMEM((1, 2 * bq), F32),
            pltpu.VMEM((1, 2 * bq), F32),
            pltpu.VMEM((1, 2 * bq), F32),
            pltpu.VMEM((1, 2 * bq), F32),
            pltpu.VMEM((DA_V_DIM, 2 * bq), F32),
        ],
        compiler_params=_params(("arbitrary", "arbitrary", "arbitrary")),
        name="attn",
    )(lq1, lk1, lq2, lk2, gsub, proj, proj, vt)


def _sgate_kernel(u_ref, v_ref, g_ref, b_ref, w_ref, bias_ref, o_ref, *, tm):
    v = v_ref[...].astype(F32)
    mu = jnp.mean(v, axis=-1, keepdims=True)
    vc = v - mu
    var = jnp.mean(vc * vc, axis=-1, keepdims=True)
    vn = (vc * lax.rsqrt(var + LN_EPS) * g_ref[...] + b_ref[...]).astype(BF16)
    i = lax.broadcasted_iota(jnp.int32, (SG_WINDOW, SG_WINDOW), 0) // CHUNK
    j = lax.broadcasted_iota(jnp.int32, (SG_WINDOW, SG_WINDOW), 1) // CHUNK
    keep = j <= i
    for g in range(SG_GROUPS):
        wm = jnp.where(keep, w_ref[g], 0.0).astype(BF16)
        cols = slice(g * SG_GROUP_DIM, (g + 1) * SG_GROUP_DIM)
        for w in range(tm // SG_WINDOW):
            rows = slice(w * SG_WINDOW, (w + 1) * SG_WINDOW)
            mixed = jnp.dot(wm, vn[rows, cols], preferred_element_type=F32) + bias_ref[:, cols]
            o_ref[rows, cols] = (u_ref[rows, cols].astype(F32) * mixed).astype(o_ref.dtype)


def _merge_value(x_ref, a_ref, s_ref, ga_ref, gs_ref, wa_ref, ws_ref, wo_ref):
    ba = jnp.dot(a_ref[...], wa_ref[...], preferred_element_type=F32)
    bs = jnp.dot(s_ref[...], ws_ref[...], preferred_element_type=F32)
    merged = ga_ref[...].astype(F32) * ba + gs_ref[...].astype(F32) * bs
    return x_ref[...] + jnp.dot(merged.astype(BF16), wo_ref[...], preferred_element_type=F32)


def _memkv_kernel(m_ref, g_ref, w_ref, o_ref):
    mn = _rms(m_ref[...], g_ref[...]).astype(BF16)
    o_ref[...] = jnp.dot(mn, w_ref[...], preferred_element_type=F32).astype(o_ref.dtype)


def _memkv(mem2, g, w, L):
    R = mem2.shape[0]
    return pl.pallas_call(
        _memkv_kernel,
        grid=(R // L,),
        in_specs=[
            pl.BlockSpec((L, D_MODEL), lambda i: (i, 0)),
            pl.BlockSpec((1, D_MODEL), lambda i: (0, 0)),
            pl.BlockSpec((D_MODEL, 2 * D_MODEL), lambda i: (0, 0)),
        ],
        out_specs=pl.BlockSpec((L, 2 * D_MODEL), lambda i: (i, 0)),
        out_shape=jax.ShapeDtypeStruct((R, 2 * D_MODEL), BF16),
        compiler_params=_params(("arbitrary",)),
        name="memkv",
    )(mem2, g, w)


def _xattn_value(h, g_ref, wq_ref, k_ref, v_ref, wo_ref):
    hn = _rms(h, g_ref[...]).astype(BF16)
    q = (jnp.dot(hn, wq_ref[...], preferred_element_type=F32) * XA_HEAD_DIM ** -0.5).astype(BF16)
    outs = []
    for hd in range(XA_HEADS):
        cols = slice(hd * XA_HEAD_DIM, (hd + 1) * XA_HEAD_DIM)
        s = lax.dot_general(q[:, cols], k_ref[:, cols], (((1,), (1,)), ((), ())), preferred_element_type=F32)
        p = jnp.exp(s - jnp.max(s, axis=-1, keepdims=True))
        l = jnp.sum(p, axis=-1, keepdims=True)
        pv = jnp.dot(p.astype(BF16), v_ref[:, cols], preferred_element_type=F32)
        outs.append((pv / l).astype(BF16))
    o = jnp.concatenate(outs, axis=-1)
    return h + jnp.dot(o, wo_ref[...], preferred_element_type=F32)


def _mix_kernel(x_ref, a_ref, u_ref, sv_ref, ga_ref, gs_ref, lng_ref, lnb_ref, wsg_ref, bias_ref,
                wa_ref, ws_ref, wo_ref, gxa_ref, wq_ref, k_ref, v_ref, wxo_ref, o_ref, sgo_scr, *, tm):
    _sgate_kernel(u_ref, sv_ref, lng_ref, lnb_ref, wsg_ref, bias_ref, sgo_scr, tm=tm)
    h1 = _merge_value(x_ref, a_ref, sgo_scr, ga_ref, gs_ref, wa_ref, ws_ref, wo_ref)
    o_ref[...] = _xattn_value(h1, gxa_ref, wq_ref, k_ref, v_ref, wxo_ref)


def _mix(x2, a, proj, ln_g, ln_b, w_s, bias, wa, ws, wo, g_xa, wq, kv, wxo, S, L, tm):
    T = x2.shape[0]
    per_b = S // tm
    row = lambda c: pl.BlockSpec((tm, D_MODEL), lambda i: (i, c))
    const = lambda shape: pl.BlockSpec(shape, lambda i: (0,) * len(shape), pipeline_mode=pl.Buffered(1))
    wspec = const((D_MODEL, D_MODEL))
    return pl.pallas_call(
        functools.partial(_mix_kernel, tm=tm),
        grid=(T // tm,),
        in_specs=[
            row(0), row(0), row(SEG_U), row(SEG_SV), row(SEG_GA), row(SEG_GS),
            const((1, D_MODEL)), const((1, D_MODEL)),
            const((SG_GROUPS, SG_WINDOW, SG_WINDOW)), const((SG_WINDOW, D_MODEL)),
            wspec, wspec, wspec,
            const((1, D_MODEL)), wspec,
            pl.BlockSpec((L, D_MODEL), lambda i: (i // per_b, 0)),
            pl.BlockSpec((L, D_MODEL), lambda i: (i // per_b, 1)),
            wspec,
        ],
        out_specs=row(0),
        out_shape=jax.ShapeDtypeStruct((T, D_MODEL), F32),
        scratch_shapes=[pltpu.VMEM((tm, D_MODEL), BF16)],
        compiler_params=_params(("arbitrary",)),
        name="mix",
    )(x2, a, proj, proj, proj, proj, ln_g, ln_b, w_s, bias, wa, ws, wo, g_xa, wq, kv, kv, wxo)


def _ffn_kernel(h_ref, g_ref, w1_ref, w2_ref, gf_ref, o_ref, *, tf, final):
    h = h_ref[...]
    n = _rms(h, g_ref[...]).astype(BF16)
    out = h
    for j in range(D_FF // tf):
        f = jnp.maximum(jnp.dot(n, w1_ref[:, j * tf:(j + 1) * tf], preferred_element_type=F32), 0.0)
        out = out + jnp.dot((f * f).astype(BF16), w2_ref[j * tf:(j + 1) * tf, :], preferred_element_type=F32)
    o_ref[...] = _rms(out, gf_ref[...]) if final else out


def _ffn(h2, g, w1, w2, gf, tm, tf, final):
    T = h2.shape[0]
    const = lambda shape: pl.BlockSpec(shape, lambda i: (0, 0), pipeline_mode=pl.Buffered(1))
    return pl.pallas_call(
        functools.partial(_ffn_kernel, tf=tf, final=final),
        grid=(T // tm,),
        in_specs=[
            pl.BlockSpec((tm, D_MODEL), lambda i: (i, 0)),
            const((1, D_MODEL)),
            const((D_MODEL, D_FF)),
            const((D_FF, D_MODEL)),
            const((1, D_MODEL)),
        ],
        out_specs=pl.BlockSpec((tm, D_MODEL), lambda i: (i, 0)),
        out_shape=jax.ShapeDtypeStruct((T, D_MODEL), F32),
        compiler_params=_params(("arbitrary",)),
        name="ffn",
    )(h2, g, w1, w2, gf)


def _transpose_kernel(w_ref, o_ref):
    o_ref[...] = w_ref[...].T.astype(o_ref.dtype)


def _transposed_block(w, col_block):
    return pl.pallas_call(
        _transpose_kernel,
        grid=(1,),
        in_specs=[pl.BlockSpec((D_MODEL, D_MODEL), lambda i: (0, col_block))],
        out_specs=pl.BlockSpec((D_MODEL, D_MODEL), lambda i: (0, 0)),
        out_shape=jax.ShapeDtypeStruct((D_MODEL, D_MODEL), BF16),
        compiler_params=_params(("arbitrary",)),
        name="wv_t",
    )(w)


def _tile(n, pref):
    t = min(n, pref)
    assert n % t == 0, (n, t)
    return t


def kernel(x, mem, positions, g_mix, w_in, lam_q1, lam_k1, lam_q2, lam_k2, g_subln, sg_ln_g, sg_ln_b, sg_w, sg_b, w_branch_attn, w_branch_sg, w_out, g_xa, g_mem, w_xq, w_xkv, w_xo, g_ffn, w_ff1, w_ff2, g_final):
    B, S, _ = x.shape
    L = mem.shape[1]
    depth = w_in.shape[0]
    T = B * S
    tm = _tile(S, 512)
    bq = _tile(S, 1024)

    idx = jnp.arange(0, ROPE_DIM, 2, dtype=F32)
    inv_freq = jnp.power(jnp.float32(ROPE_THETA), -idx / ROPE_DIM)
    d = jnp.arange(LANES) % DA_HEAD_DIM
    invf = jnp.where(d < ROPE_DIM, inv_freq[d % ROPE_HALF], 0.0).reshape(1, LANES)
    pos = positions.reshape(T, 1).astype(F32)

    row = lambda p: p.reshape(1, -1).astype(F32)
    h = x.reshape(T, D_MODEL)
    mem2 = mem.reshape(B * L, D_MODEL)
    for l in range(depth):
        lam_init = 0.8 - 0.6 * math.exp(-0.3 * l)
        w = w_in[l].astype(BF16)
        w_vt = _transposed_block(w_in[l], 2)
        proj, vt = _in_proj(h, pos, invf, row(g_mix[l]), w, w_vt, B, S, tm)
        a = _attn(proj, vt, row(lam_q1[l]), row(lam_k1[l]), row(lam_q2[l]), row(lam_k2[l]),
                  g_subln[l].reshape(DA_V_DIM, 1).astype(F32), B, S, bq, lam_init)
        bias = jnp.repeat(sg_b[l].T.astype(F32), SG_GROUP_DIM, axis=1)
        kv = _memkv(mem2, row(g_mem[l]), w_xkv[l].astype(BF16), L)
        h2 = _mix(h, a, proj, row(sg_ln_g[l]), row(sg_ln_b[l]), sg_w[l].astype(F32), bias,
                  w_branch_attn[l].astype(BF16), w_branch_sg[l].astype(BF16), w_out[l].astype(BF16),
                  row(g_xa[l]), w_xq[l].astype(BF16), kv, w_xo[l].astype(BF16), S, L, tm)
        h = _ffn(h2, row(g_ffn[l]), w_ff1[l].astype(BF16), w_ff2[l].astype(BF16), row(g_final),
                 tm, _tile(D_FF, 1024), final=(l == depth - 1))
    return h.reshape(B, S, D_MODEL)
```

```python
import functools
import math

import jax
import jax.numpy as jnp
from jax import lax
from jax.experimental import pallas as pl
from jax.experimental.pallas import tpu as pltpu

F32 = jnp.float32
BF16 = jnp.bfloat16

D_MODEL = 1024
CHUNK = 64
EPS = 1e-6
LN_EPS = 1e-5
DA_HEADS = 8
DA_HEAD_DIM = 64
DA_V_DIM = 128
ROPE_THETA = 500000.0
ROPE_DIM = 16
ROPE_HALF = ROPE_DIM // 2
SG_GROUPS = 8
SG_GROUP_DIM = 128
SG_WINDOW = 128
XA_HEADS = 4
XA_HEAD_DIM = 256
D_FF = 4096
N_SEG = 6
SEG_Q, SEG_K, SEG_U, SEG_SV, SEG_GA, SEG_GS = range(N_SEG)

LANES = 128
VMEM_LIMIT = 56 * 1024 * 1024


def _rms(x, g):
    return x * lax.rsqrt(jnp.mean(x * x, axis=-1, keepdims=True) + EPS) * g


def _params(sem, vmem=VMEM_LIMIT, flags=None):
    return pltpu.CompilerParams(dimension_semantics=sem, vmem_limit_bytes=vmem, flags=flags)


def _in_proj_kernel(x_ref, pos_ref, invf_ref, g_ref, w_ref, wvt_ref, o_ref, vt_ref):
    n = _rms(x_ref[...], g_ref[...]).astype(BF16)
    ang = pos_ref[...] * invf_ref[...]
    d = lax.broadcasted_iota(jnp.int32, (1, LANES), 1) % DA_HEAD_DIM
    lo = d < ROPE_HALF
    hi = (d >= ROPE_HALF) & (d < ROPE_DIM)
    cos = jnp.cos(ang)
    sin = jnp.sin(ang)
    c = jnp.where(lo | hi, cos, 1.0)
    s1 = jnp.where(lo, -sin, 0.0)
    s2 = jnp.where(hi, sin, 0.0)

    def seg(j):
        cols = slice(j * D_MODEL, (j + 1) * D_MODEL)
        jw = j + (j > SEG_K)
        return cols, jnp.dot(n, w_ref[:, jw * D_MODEL:(jw + 1) * D_MODEL], preferred_element_type=F32)

    def rope(j, scale):
        cols, acc = seg(j)
        for h in range(DA_HEADS):
            sl = acc[:, h * LANES:(h + 1) * LANES]
            r = sl * c + pltpu.roll(sl, LANES - ROPE_HALF, 1) * s1 + pltpu.roll(sl, ROPE_HALF, 1) * s2
            o_ref[:, j * D_MODEL + h * LANES:j * D_MODEL + (h + 1) * LANES] = (r * scale).astype(o_ref.dtype)

    rope(SEG_Q, DA_HEAD_DIM ** -0.5 * math.log2(math.e))
    rope(SEG_K, 1.0)
    for j in (SEG_U, SEG_SV):
        cols, acc = seg(j)
        o_ref[:, cols] = jax.nn.gelu(acc).astype(o_ref.dtype)
    for j in (SEG_GA, SEG_GS):
        cols, acc = seg(j)
        o_ref[:, cols] = jax.nn.sigmoid(acc).astype(o_ref.dtype)
    vt = lax.dot_general(wvt_ref[...], n, (((1,), (1,)), ((), ())), preferred_element_type=F32)
    for h in range(DA_HEADS):
        vt_ref[h, 0] = vt[h * DA_V_DIM:(h + 1) * DA_V_DIM, :].astype(vt_ref.dtype)


def _in_proj(x2, pos, invf, g, w, wvt, B, S, tm):
    T = x2.shape[0]
    nk = S // tm
    const = lambda shape: pl.BlockSpec(shape, lambda i: (0, 0), pipeline_mode=pl.Buffered(1))
    return pl.pallas_call(
        _in_proj_kernel,
        grid=(T // tm,),
        in_specs=[
            pl.BlockSpec((tm, D_MODEL), lambda i: (i, 0)),
            pl.BlockSpec((tm, 1), lambda i: (i, 0)),
            const((1, LANES)),
            const((1, D_MODEL)),
            const((D_MODEL, (N_SEG + 1) * D_MODEL)),
            const((D_MODEL, D_MODEL)),
        ],
        out_specs=[
            pl.BlockSpec((tm, N_SEG * D_MODEL), lambda i: (i, 0)),
            pl.BlockSpec((DA_HEADS, 1, DA_V_DIM, tm), lambda i: (i // nk, i % nk, 0, 0)),
        ],
        out_shape=[
            jax.ShapeDtypeStruct((T, N_SEG * D_MODEL), BF16),
            jax.ShapeDtypeStruct((B * DA_HEADS, nk, DA_V_DIM, tm), BF16),
        ],
        compiler_params=_params(("arbitrary",)),
        name="in_proj",
    )(x2, pos, invf, g, w, wvt)


def _attn_kernel(lq1_ref, lk1_ref, lq2_ref, lk2_ref, gsub_ref, q_ref, k_ref, vt_ref, o_ref,
                 sa_scr, sb_scr, mca_scr, mcb_scr, m_scr, l_scr, acc_scr, *, bq, lam_init):
    qi = pl.program_id(2)
    q = q_ref[...]
    lane = lax.broadcasted_iota(jnp.int32, (1, LANES), 1)
    zero = jnp.zeros_like(q)
    qs = jnp.concatenate([jnp.where(lane < DA_HEAD_DIM, q, zero),
                          jnp.where(lane >= DA_HEAD_DIM, q, zero)], axis=0)

    m_scr[...] = jnp.full(m_scr.shape, -jnp.inf, F32)
    l_scr[...] = jnp.zeros(l_scr.shape, F32)
    acc_scr[...] = jnp.zeros(acc_scr.shape, F32)

    def scores(kb, buf, masked):
        s_ref, mc_ref = buf
        k = k_ref[pl.ds(pl.multiple_of(kb * bq, bq), bq), :]
        st = lax.dot_general(k, qs, (((1,), (1,)), ((), ())), preferred_element_type=F32)
        if masked:
            key = lax.broadcasted_iota(jnp.int32, (bq, 2 * bq), 0)
            qry = lax.broadcasted_iota(jnp.int32, (bq, 2 * bq), 1) % bq
            st = jnp.where(key // CHUNK <= qry // CHUNK, st, -jnp.inf)
        s_ref[...] = st
        mc_ref[...] = jnp.max(st, axis=0, keepdims=True)

    def accumulate(kb, buf):
        s_ref, mc_ref = buf
        m_prev = m_scr[...]
        m_new = jnp.maximum(m_prev, mc_ref[...])
        alpha = jnp.exp2(m_prev - m_new)
        p = jnp.exp2(s_ref[...] - m_new)
        l_scr[...] = alpha * l_scr[...] + jnp.sum(p, axis=0, keepdims=True)
        pb = p.astype(BF16)
        vb = vt_ref.shape[-1]
        pv = sum(jnp.dot(vt_ref[0, kb * (bq // vb) + i], pb[i * vb:(i + 1) * vb],
                         preferred_element_type=F32) for i in range(bq // vb))
        acc_scr[...] = alpha * acc_scr[...] + pv
        m_scr[...] = m_new

    buf_a, buf_b = (sa_scr, mca_scr), (sb_scr, mcb_scr)
    scores(qi, buf_a, True)

    def pair(u, carry):
        t = 2 * u
        scores(t, buf_b, False)
        accumulate(jnp.where(t == 0, qi, t - 1), buf_a)
        scores(t + 1, buf_a, False)
        accumulate(t, buf_b)
        return carry

    lax.fori_loop(0, qi // 2, pair, 0)

    @pl.when(qi % 2 == 1)
    def _():
        t = qi - 1
        scores(t, buf_b, False)
        accumulate(jnp.where(t == 0, qi, t - 1), buf_a)
        accumulate(t, buf_b)

    @pl.when(qi % 2 == 0)
    def _():
        accumulate(jnp.where(qi == 0, qi, qi - 1), buf_a)

    lam = (jnp.exp(jnp.sum(lq1_ref[...] * lk1_ref[...], keepdims=True))
           - jnp.exp(jnp.sum(lq2_ref[...] * lk2_ref[...], keepdims=True)) + lam_init)
    ot = acc_scr[...] / l_scr[...]
    at = ot[:, :bq] - lam * ot[:, bq:]
    ms = jnp.mean(at * at, axis=0, keepdims=True)
    at = at * lax.rsqrt(ms + EPS) * gsub_ref[...] * (1.0 - lam_init)
    o_ref[...] = at.T.astype(o_ref.dtype)


def _attn(proj, vt, lq1, lk1, lq2, lk2, gsub, B, S, bq, lam_init):
    T = B * S
    nq = S // bq
    small = lambda shape: pl.BlockSpec(shape, lambda b, h, i: (0, 0))
    return pl.pallas_call(
        functools.partial(_attn_kernel, bq=bq, lam_init=lam_init),
        grid=(B, DA_HEADS, nq),
        in_specs=[
            small((1, DA_HEAD_DIM)), small((1, DA_HEAD_DIM)), small((1, DA_HEAD_DIM)), small((1, DA_HEAD_DIM)),
            small((DA_V_DIM, 1)),
            pl.BlockSpec((bq, LANES), lambda b, h, i: (b * nq + i, SEG_Q * DA_HEADS + h)),
            pl.BlockSpec((S, LANES), lambda b, h, i: (b, SEG_K * DA_HEADS + h)),
            pl.BlockSpec((1,) + vt.shape[1:], lambda b, h, i: (b * DA_HEADS + h, 0, 0, 0)),
        ],
        out_specs=pl.BlockSpec((bq, LANES), lambda b, h, i: (b * nq + i, h)),
        out_shape=jax.ShapeDtypeStruct((T, D_MODEL), BF16),
        scratch_shapes=[
            pltpu.VMEM((bq, 2 * bq), F32),
            pltpu.VMEM((bq, 2 * bq), F32),
            pltpu.VMEM((1, 2 * bq), F32),
            pltpu.VMEM((1, 2 * bq), F32),
            pltpu.VMEM((1, 2 * bq), F32),
            pltpu.VMEM((1, 2 * bq), F32),
            pltpu.VMEM((DA_V_DIM, 2 * bq), F32),
        ],
        compiler_params=_params(("arbitrary", "arbitrary", "arbitrary")),
        name="attn",
    )(lq1, lk1, lq2, lk2, gsub, proj, proj, vt)


def _sgate_kernel(u_ref, v_ref, g_ref, b_ref, w_ref, bias_ref, o_ref, *, tm):
    v = v_ref[...].astype(F32)
    mu = jnp.mean(v, axis=-1, keepdims=True)
    vc = v - mu
    var = jnp.mean(vc * vc, axis=-1, keepdims=True)
    vn = (vc * lax.rsqrt(var + LN_EPS) * g_ref[...] + b_ref[...]).astype(BF16)
    i = lax.broadcasted_iota(jnp.int32, (SG_WINDOW, SG_WINDOW), 0) // CHUNK
    j = lax.broadcasted_iota(jnp.int32, (SG_WINDOW, SG_WINDOW), 1) // CHUNK
    keep = j <= i
    for g in range(SG_GROUPS):
        wm = jnp.where(keep, w_ref[g], 0.0).astype(BF16)
        cols = slice(g * SG_GROUP_DIM, (g + 1) * SG_GROUP_DIM)
        for w in range(tm // SG_WINDOW):
            rows = slice(w * SG_WINDOW, (w + 1) * SG_WINDOW)
            mixed = jnp.dot(wm, vn[rows, cols], preferred_element_type=F32) + bias_ref[:, cols]
            o_ref[rows, cols] = (u_ref[rows, cols].astype(F32) * mixed).astype(o_ref.dtype)


def _merge_value(x_ref, a_ref, s_ref, ga_ref, gs_ref, wa_ref, ws_ref, wo_ref):
    ba = jnp.dot(a_ref[...], wa_ref[...], preferred_element_type=F32)
    bs = jnp.dot(s_ref[...], ws_ref[...], preferred_element_type=F32)
    merged = ga_ref[...].astype(F32) * ba + gs_ref[...].astype(F32) * bs
    return x_ref[...] + jnp.dot(merged.astype(BF16), wo_ref[...], preferred_element_type=F32)


def _memkv_kernel(m_ref, g_ref, w_ref, o_ref):
    mn = _rms(m_ref[...], g_ref[...]).astype(BF16)
    o_ref[...] = jnp.dot(mn, w_ref[...], preferred_element_type=F32).astype(o_ref.dtype)


def _memkv(mem2, g, w, L):
    R = mem2.shape[0]
    return pl.pallas_call(
        _memkv_kernel,
        grid=(R // L,),
        in_specs=[
            pl.BlockSpec((L, D_MODEL), lambda i: (i, 0)),
            pl.BlockSpec((1, D_MODEL), lambda i: (0, 0)),
            pl.BlockSpec((D_MODEL, 2 * D_MODEL), lambda i: (0, 0)),
        ],
        out_specs=pl.BlockSpec((L, 2 * D_MODEL), lambda i: (i, 0)),
        out_shape=jax.ShapeDtypeStruct((R, 2 * D_MODEL), BF16),
        compiler_params=_params(("arbitrary",)),
        name="memkv",
    )(mem2, g, w)


def _xattn_value(h, g_ref, wq_ref, k_ref, v_ref, wo_ref):
    hn = _rms(h, g_ref[...]).astype(BF16)
    q = (jnp.dot(hn, wq_ref[...], preferred_element_type=F32) * XA_HEAD_DIM ** -0.5).astype(BF16)
    outs = []
    for hd in range(XA_HEADS):
        cols = slice(hd * XA_HEAD_DIM, (hd + 1) * XA_HEAD_DIM)
        s = lax.dot_general(q[:, cols], k_ref[:, cols], (((1,), (1,)), ((), ())), preferred_element_type=F32)
        p = jnp.exp(s - jnp.max(s, axis=-1, keepdims=True))
        l = jnp.sum(p, axis=-1, keepdims=True)
        pv = jnp.dot(p.astype(BF16), v_ref[:, cols], preferred_element_type=F32)
        outs.append((pv / l).astype(BF16))
    o = jnp.concatenate(outs, axis=-1)
    return h + jnp.dot(o, wo_ref[...], preferred_element_type=F32)


def _mix_kernel(x_ref, a_ref, u_ref, sv_ref, ga_ref, gs_ref, lng_ref, lnb_ref, wsg_ref, bias_ref,
                wa_ref, ws_ref, wo_ref, gxa_ref, wq_ref, k_ref, v_ref, wxo_ref,
                gffn_ref, w1_ref, w2_ref, gf_ref, o_ref, sgo_scr, *, tm, tf, final):
    _sgate_kernel(u_ref, sv_ref, lng_ref, lnb_ref, wsg_ref, bias_ref, sgo_scr, tm=tm)
    h1 = _merge_value(x_ref, a_ref, sgo_scr, ga_ref, gs_ref, wa_ref, ws_ref, wo_ref)
    h2 = _xattn_value(h1, gxa_ref, wq_ref, k_ref, v_ref, wxo_ref)
    o_ref[...] = _ffn_value(h2, gffn_ref, w1_ref, w2_ref, gf_ref, tf, final)


def _mix(x2, a, proj, ln_g, ln_b, w_s, bias, wa, ws, wo, g_xa, wq, kv, wxo, g_ffn, w1, w2, gf,
         S, L, tm, tf, final):
    T = x2.shape[0]
    per_b = S // tm
    row = lambda c: pl.BlockSpec((tm, D_MODEL), lambda i: (i, c))
    const = lambda shape: pl.BlockSpec(shape, lambda i: (0,) * len(shape), pipeline_mode=pl.Buffered(1))
    wspec = const((D_MODEL, D_MODEL))
    return pl.pallas_call(
        functools.partial(_mix_kernel, tm=tm, tf=tf, final=final),
        grid=(T // tm,),
        in_specs=[
            row(0), row(0), row(SEG_U), row(SEG_SV), row(SEG_GA), row(SEG_GS),
            const((1, D_MODEL)), const((1, D_MODEL)),
            const((SG_GROUPS, SG_WINDOW, SG_WINDOW)), const((SG_WINDOW, D_MODEL)),
            wspec, wspec, wspec,
            const((1, D_MODEL)), wspec,
            pl.BlockSpec((L, D_MODEL), lambda i: (i // per_b, 0)),
            pl.BlockSpec((L, D_MODEL), lambda i: (i // per_b, 1)),
            wspec,
            const((1, D_MODEL)), const((D_MODEL, D_FF)), const((D_FF, D_MODEL)), const((1, D_MODEL)),
        ],
        out_specs=row(0),
        out_shape=jax.ShapeDtypeStruct((T, D_MODEL), F32),
        scratch_shapes=[pltpu.VMEM((tm, D_MODEL), BF16)],
        compiler_params=_params(("arbitrary",)),
        name="mix",
    )(x2, a, proj, proj, proj, proj, ln_g, ln_b, w_s, bias, wa, ws, wo, g_xa, wq, kv, kv, wxo,
      g_ffn, w1, w2, gf)


def _ffn_value(h, g_ref, w1_ref, w2_ref, gf_ref, tf, final):
    n = _rms(h, g_ref[...]).astype(BF16)
    out = h
    for j in range(D_FF // tf):
        f = jnp.maximum(jnp.dot(n, w1_ref[:, j * tf:(j + 1) * tf], preferred_element_type=F32), 0.0)
        out = out + jnp.dot((f * f).astype(BF16), w2_ref[j * tf:(j + 1) * tf, :], preferred_element_type=F32)
    return _rms(out, gf_ref[...]) if final else out


def _transpose_kernel(w_ref, o_ref):
    o_ref[...] = w_ref[...].T.astype(o_ref.dtype)


def _transposed_block(w, col_block):
    return pl.pallas_call(
        _transpose_kernel,
        grid=(1,),
        in_specs=[pl.BlockSpec((D_MODEL, D_MODEL), lambda i: (0, col_block))],
        out_specs=pl.BlockSpec((D_MODEL, D_MODEL), lambda i: (0, 0)),
        out_shape=jax.ShapeDtypeStruct((D_MODEL, D_MODEL), BF16),
        compiler_params=_params(("arbitrary",)),
        name="wv_t",
    )(w)


def _tile(n, pref):
    t = min(n, pref)
    assert n % t == 0, (n, t)
    return t


def kernel(x, mem, positions, g_mix, w_in, lam_q1, lam_k1, lam_q2, lam_k2, g_subln, sg_ln_g, sg_ln_b, sg_w, sg_b, w_branch_attn, w_branch_sg, w_out, g_xa, g_mem, w_xq, w_xkv, w_xo, g_ffn, w_ff1, w_ff2, g_final):
    B, S, _ = x.shape
    L = mem.shape[1]
    depth = w_in.shape[0]
    T = B * S
    tm = _tile(S, 512)
    bq = _tile(S, 1024)

    idx = jnp.arange(0, ROPE_DIM, 2, dtype=F32)
    inv_freq = jnp.power(jnp.float32(ROPE_THETA), -idx / ROPE_DIM)
    d = jnp.arange(LANES) % DA_HEAD_DIM
    invf = jnp.where(d < ROPE_DIM, inv_freq[d % ROPE_HALF], 0.0).reshape(1, LANES)
    pos = positions.reshape(T, 1).astype(F32)

    row = lambda p: p.reshape(1, -1).astype(F32)
    h = x.reshape(T, D_MODEL)
    mem2 = mem.reshape(B * L, D_MODEL)
    for l in range(depth):
        lam_init = 0.8 - 0.6 * math.exp(-0.3 * l)
        w = w_in[l].astype(BF16)
        w_vt = _transposed_block(w_in[l], 2)
        proj, vt = _in_proj(h, pos, invf, row(g_mix[l]), w, w_vt, B, S, tm)
        a = _attn(proj, vt, row(lam_q1[l]), row(lam_k1[l]), row(lam_q2[l]), row(lam_k2[l]),
                  g_subln[l].reshape(DA_V_DIM, 1).astype(F32), B, S, bq, lam_init)
        bias = jnp.repeat(sg_b[l].T.astype(F32), SG_GROUP_DIM, axis=1)
        kv = _memkv(mem2, row(g_mem[l]), w_xkv[l].astype(BF16), L)
        h = _mix(h, a, proj, row(sg_ln_g[l]), row(sg_ln_b[l]), sg_w[l].astype(F32), bias,
                 w_branch_attn[l].astype(BF16), w_branch_sg[l].astype(BF16), w_out[l].astype(BF16),
                 row(g_xa[l]), w_xq[l].astype(BF16), kv, w_xo[l].astype(BF16),
                 row(g_ffn[l]), w_ff1[l].astype(BF16), w_ff2[l].astype(BF16), row(g_final),
                 S, L, tm, _tile(D_FF, 1024), final=(l == depth - 1))
    return h.reshape(B, S, D_MODEL)
```
